```python
import math
import jax
import jax.numpy as jnp
from jax import lax
import numpy as np

D_MODEL = 2048
BATCH = 1
SEQ = 8192
DEPTH = 4

NH_M = 8
DH_M = D_MODEL // NH_M
DV_M = D_MODEL // NH_M
CONV_K = 4
CHUNK = 128
DH_A = 128
NH_A = D_MODEL // DH_A
NKV_A = NH_A // 4
NH_I = 16
DI = 64
TOPK_MAX = 256
QBLK = 128
N_BUCKETS = 32
MAX_DIST = 128
N_GROUPS = 4
EXP_PER_GROUP = 8
N_EXPERTS = N_GROUPS * EXP_PER_GROUP
TOP_K = 2
D_FF_E = D_MODEL // 4
MOE_BLK = 128
EPS = 1e-6

IN_SIZES = (2 * NH_M * DH_M, NH_M * DV_M, NH_M * DV_M, NH_M, NH_M,
            NH_A * DH_A, NKV_A * DH_A, NKV_A * DH_A,
            NH_I * DI, DI, NH_I, D_MODEL, D_MODEL)
D_IN = sum(IN_SIZES)

kernel_name = "hybrid_mlstm_dsa_hiermoe_trunk"


def rms(x):
    xf = x.astype(jnp.float32)
    return (xf * lax.rsqrt(jnp.mean(xf * xf, axis=-1, keepdims=True) + EPS)).astype(x.dtype)


def causal_conv(u, w):
    S = u.shape[1]
    K = w.shape[0]
    up = jnp.pad(u, ((0, 0), (K - 1, 0), (0, 0)))
    out = up[:, 0:S] * w[0]
    for j in range(1, K):
        out = out + up[:, j:j + S] * w[j]
    return out


def t5_bucket(dist):
    n = jnp.maximum(dist, 0)
    max_exact = N_BUCKETS // 2
    nf = jnp.maximum(n, max_exact).astype(jnp.float32)
    large = max_exact + (jnp.log(nf / max_exact) / math.log(MAX_DIST / max_exact)
                         * (N_BUCKETS - max_exact)).astype(jnp.int32)
    large = jnp.minimum(large, N_BUCKETS - 1)
    return jnp.where(n < max_exact, n, large)


def mlstm(q, k, v, i_pre, f_pre):
    B, S, H, dk = q.shape
    dv = v.shape[-1]
    L = CHUNK
    NC = S // L
    f32 = jnp.float32

    def chunks(t):
        return t.astype(f32).reshape(B, NC, L, H, -1).transpose(0, 3, 1, 2, 4)

    qc = chunks(q)
    kc = chunks(k) * (dk ** -0.5)
    vc = chunks(v)
    ig = i_pre.astype(f32).reshape(B, NC, L, H).transpose(0, 3, 1, 2)
    lf = jax.nn.log_sigmoid(f_pre.astype(f32)).reshape(B, NC, L, H).transpose(0, 3, 1, 2)
    b = jnp.cumsum(lf, axis=-1)
    g = b[..., -1]
    causal = jnp.tril(jnp.ones((L, L), bool))
    dmat = jnp.where(causal, b[..., :, None] - b[..., None, :] + ig[..., None, :], -jnp.inf)
    m_intra = jnp.max(dmat, axis=-1)
    a = g[..., None] - b + ig
    m_loc = jnp.max(a, axis=-1)
    w_loc = jnp.exp(a - m_loc[..., None])
    c_loc = jnp.einsum('bhcld,bhcle->bhcde', kc * w_loc[..., None], vc)
    n_loc = jnp.einsum('bhcl,bhcld->bhcd', w_loc, kc)

    def step(carry, inp):
        c_st, n_st, m_st = carry
        c_l, n_l, m_l, g_c = inp
        m_new = jnp.maximum(g_c + m_st, m_l)
        s_p = jnp.exp(g_c + m_st - m_new)
        s_l = jnp.exp(m_l - m_new)
        c_new = s_p[..., None, None] * c_st + s_l[..., None, None] * c_l
        n_new = s_p[..., None] * n_st + s_l[..., None] * n_l
        return (c_new, n_new, m_new), (c_st, n_st, m_st)

    init = (jnp.zeros((B, H, dk, dv), f32), jnp.zeros((B, H, dk), f32), jnp.zeros((B, H), f32))
    xs = (c_loc.transpose(2, 0, 1, 3, 4), n_loc.transpose(2, 0, 1, 3),
          m_loc.transpose(2, 0, 1), g.transpose(2, 0, 1))
    _, (c_prev, n_prev, m_prev) = lax.scan(step, init, xs)
    c_prev = c_prev.transpose(1, 2, 0, 3, 4)
    n_prev = n_prev.transpose(1, 2, 0, 3)
    m_prev = m_prev.transpose(1, 2, 0)
    inter = b + m_prev[..., None]
    m_t = jnp.maximum(inter, m_intra)
    s_inter = jnp.exp(inter - m_t)
    w_intra = jnp.exp(dmat - m_t[..., None]) * jnp.einsum('bhcld,bhcsd->bhcls', qc, kc)
    num = (jnp.einsum('bhcls,bhcse->bhcle', w_intra, vc)
           + s_inter[..., None] * jnp.einsum('bhcld,bhcde->bhcle', qc, c_prev))
    den = jnp.sum(w_intra, axis=-1) + s_inter * jnp.einsum('bhcld,bhcd->bhcl', qc, n_prev)
    h = num / jnp.maximum(jnp.abs(den), jnp.exp(-m_t))[..., None]
    return h.transpose(0, 2, 3, 1, 4).reshape(B, S, H, dv).astype(v.dtype)


def dsa_attention(q, k, v, q_idx, k_idx, w_idx, rel_bias):
    B, S, H, dh = q.shape
    hkv = k.shape[2]
    grp = H // hkv
    topk = min(TOPK_MAX, S // 4)
    nb = S // QBLK
    key_pos = jnp.arange(S, dtype=jnp.int32)
    bidx = jnp.arange(B)[:, None, None]

    def block(args):
        qb, qib, wb, t0 = args
        tq = t0 + jnp.arange(QBLK, dtype=jnp.int32)
        sc = jax.nn.relu(jnp.einsum('bqhd,bsd->bqhs', qib, k_idx))
        sc = jnp.einsum('bqhs,bqh->bqs', sc, wb).astype(jnp.float32)
        sc = jnp.where((key_pos[None, :] <= tq[:, None])[None], sc, -jnp.inf)
        _, idx = lax.top_k(sc, topk)
        ks = k[bidx, idx]
        vs = v[bidx, idx]
        logits = jnp.einsum('bqngd,bqknd->bqngk', qb.reshape(B, QBLK, hkv, grp, dh),
                            ks).astype(jnp.float32) * (dh ** -0.5)
        dist = tq[None, :, None] - idx
        bias = rel_bias[t5_bucket(dist)].astype(jnp.float32)
        bias = bias.reshape(B, QBLK, topk, hkv, grp).transpose(0, 1, 3, 4, 2)
        valid = (dist >= 0)[:, :, None, None, :]
        p = jax.nn.softmax(jnp.where(valid, logits + bias, -jnp.inf), axis=-1)
        out = jnp.einsum('bqngk,bqknd->bqngd', p.astype(vs.dtype), vs)
        return out.reshape(B, QBLK, H, dh)

    def blocks(t):
        return t.reshape(B, nb, QBLK, *t.shape[2:]).swapaxes(0, 1)

    out = lax.map(block, (blocks(q), blocks(q_idx), blocks(w_idx),
                          jnp.arange(nb, dtype=jnp.int32) * QBLK))
    return out.swapaxes(0, 1).reshape(B, S, H * dh)


def hybrid_mixer(h, w_in, conv_w, b_gate, mlstm_norm, q_norm, k_norm, rel_bias,
                 w_proj_m, w_proj_a, w_out):
    B, S, _ = h.shape
    z = h @ w_in
    cuts = [int(t) for t in np.cumsum(IN_SIZES)[:-1]]
    qk_raw, vm, om, ip, fp, qa, ka, va, qi, ki, wi, gm, ga = jnp.split(z, cuts, axis=-1)
    qk = jax.nn.silu(causal_conv(qk_raw, conv_w))
    qm, km = jnp.split(qk, 2, axis=-1)
    ip = ip + b_gate[:NH_M]
    fp = fp + b_gate[NH_M:]
    hm = mlstm(qm.reshape(B, S, NH_M, DH_M), km.reshape(B, S, NH_M, DH_M),
               vm.reshape(B, S, NH_M, DV_M), ip, fp)
    ym = rms(hm).reshape(B, S, NH_M * DV_M) * mlstm_norm * jax.nn.sigmoid(om)
    qa = rms(qa.reshape(B, S, NH_A, DH_A)) * q_norm
    ka = rms(ka.reshape(B, S, NKV_A, DH_A)) * k_norm
    va = va.reshape(B, S, NKV_A, DH_A)
    wi = wi * ((NH_I * DI) ** -0.5)
    ya = dsa_attention(qa, ka, va, qi.reshape(B, S, NH_I, DI), ki, wi, rel_bias)
    merged = jax.nn.sigmoid(gm) * (ym @ w_proj_m) + jax.nn.sigmoid(ga) * (ya @ w_proj_a)
    return merged @ w_out


def hier_moe(h, w_grp, b_grp, w_exp, b_exp, w1, w3, w2):
    B, S, D = h.shape
    n = B * S
    hf = h.reshape(n, D)
    g_logit = (hf @ w_grp + b_grp).astype(jnp.float32)
    g_sel = jnp.argmax(g_logit, axis=-1)
    p_grp = jnp.take_along_axis(jax.nn.softmax(g_logit, axis=-1), g_sel[:, None], axis=-1)
    e_logit = (hf @ w_exp + b_exp).astype(jnp.float32).reshape(n, N_GROUPS, EXP_PER_GROUP)
    e_logit = jnp.take_along_axis(e_logit, g_sel[:, None, None], axis=1)[:, 0]
    top_v, top_i = lax.top_k(e_logit, TOP_K)
    gate = p_grp * jax.nn.softmax(top_v, axis=-1)
    expert = (g_sel[:, None] * EXP_PER_GROUP + top_i).reshape(-1).astype(jnp.int32)
    m = n * TOP_K
    tok = jnp.repeat(jnp.arange(n, dtype=jnp.int32), TOP_K)
    order = jnp.argsort(expert)
    e_s = expert[order]
    tok_s = tok[order]
    gate_s = gate.reshape(-1)[order]
    counts = jnp.bincount(expert, length=N_EXPERTS)
    offs = jnp.cumsum(counts) - counts
    pcounts = (counts + MOE_BLK - 1) // MOE_BLK * MOE_BLK
    pends = jnp.cumsum(pcounts)
    dest = pends[e_s] - pcounts[e_s] + jnp.arange(m, dtype=jnp.int32) - offs[e_s]
    nblk = -(-m // MOE_BLK) + N_EXPERTS
    rows = nblk * MOE_BLK
    row_tok = jnp.zeros((rows,), jnp.int32).at[dest].set(tok_s)
    row_gate = jnp.zeros((rows,), h.dtype).at[dest].set(gate_s.astype(h.dtype))
    blk_exp = jnp.minimum(jnp.searchsorted(pends, jnp.arange(nblk, dtype=jnp.int32) * MOE_BLK,
                                           side='right'), N_EXPERTS - 1)

    def run(args):
        e, toks = args
        xb = hf[toks]
        return (jax.nn.silu(xb @ w1[e]) * (xb @ w3[e])) @ w2[e]

    yb = lax.map(run, (blk_exp, row_tok.reshape(nblk, MOE_BLK)))
    y = jax.ops.segment_sum(yb.reshape(rows, D) * row_gate[:, None], row_tok, num_segments=n)
    return y.reshape(B, S, D)


def setup_inputs(seed: int = 0) -> dict:
    key = jax.random.key(seed)
    ks = jax.random.split(key, 24)
    D = D_MODEL

    def nrm(k, shape, s):
        return jax.random.normal(k, shape, jnp.float32) * s

    b_gate = jnp.concatenate([nrm(ks[8], (DEPTH, NH_M), 0.1),
                              jnp.linspace(3.0, 6.0, NH_M, dtype=jnp.float32)[None]
                              + nrm(ks[9], (DEPTH, NH_M), 0.1)], axis=-1)
    return {
        "x": nrm(ks[0], (BATCH, SEQ, D), 1.0),
        "c": nrm(ks[1], (BATCH, D), 1.0),
        "w_ada": nrm(ks[2], (DEPTH, D, 6 * D), 0.5 * D ** -0.5),
        "b_ada": nrm(ks[3], (DEPTH, 6 * D), 0.01),
        "norm1": 1.0 + nrm(ks[4], (DEPTH, D), 0.02),
        "norm2": 1.0 + nrm(ks[5], (DEPTH, D), 0.02),
        "w_in": nrm(ks[6], (DEPTH, D, D_IN), D ** -0.5),
        "conv_w": nrm(ks[7], (DEPTH, CONV_K, 2 * NH_M * DH_M), CONV_K ** -0.5),
        "b_gate": b_gate,
        "mlstm_norm": 1.0 + nrm(ks[10], (DEPTH, NH_M * DV_M), 0.02),
        "q_norm": 1.0 + nrm(ks[11], (DEPTH, DH_A), 0.02),
        "k_norm": 1.0 + nrm(ks[12], (DEPTH, DH_A), 0.02),
        "rel_bias": nrm(ks[13], (N_BUCKETS, NH_A), 0.5),
        "w_proj_m": nrm(ks[14], (DEPTH, NH_M * DV_M, D), (NH_M * DV_M) ** -0.5),
        "w_proj_a": nrm(ks[15], (DEPTH, NH_A * DH_A, D), (NH_A * DH_A) ** -0.5),
        "w_out": nrm(ks[16], (DEPTH, D, D), D ** -0.5),
        "w_grp": nrm(ks[17], (DEPTH, D, N_GROUPS), D ** -0.5),
        "b_grp": nrm(ks[18], (DEPTH, N_GROUPS), 0.01),
        "w_exp": nrm(ks[19], (DEPTH, D, N_EXPERTS), D ** -0.5),
        "b_exp": nrm(ks[20], (DEPTH, N_EXPERTS), 0.01),
        "w1": nrm(ks[21], (DEPTH, N_EXPERTS, D, D_FF_E), D ** -0.5),
        "w3": nrm(ks[22], (DEPTH, N_EXPERTS, D, D_FF_E), D ** -0.5),
        "w2": nrm(ks[23], (DEPTH, N_EXPERTS, D_FF_E, D), D_FF_E ** -0.5),
    }


def reference(x, c, w_ada, b_ada, norm1, norm2, w_in, conv_w, b_gate, mlstm_norm, q_norm,
              k_norm, rel_bias, w_proj_m, w_proj_a, w_out, w_grp, b_grp, w_exp, b_exp,
              w1, w3, w2):
    c_act = jax.nn.silu(c)
    for l in range(DEPTH):
        mod = c_act @ w_ada[l] + b_ada[l]
        sh1, sc1, g1, sh2, sc2, g2 = jnp.split(mod[:, None, :], 6, axis=-1)
        h = rms(x) * norm1[l] * (1.0 + sc1) + sh1
        x = x + g1 * hybrid_mixer(h, w_in[l], conv_w[l], b_gate[l], mlstm_norm[l], q_norm[l],
                                  k_norm[l], rel_bias, w_proj_m[l], w_proj_a[l], w_out[l])
        h = rms(x) * norm2[l] * (1.0 + sc2) + sh2
        x = x + g2 * hier_moe(h, w_grp[l], b_grp[l], w_exp[l], b_exp[l], w1[l], w3[l], w2[l])
    return x
```

```python
import functools
import math

import numpy as np
import jax
import jax.numpy as jnp
from jax import lax
from jax.experimental import pallas as pl
from jax.experimental.pallas import tpu as pltpu

F32 = jnp.float32
BF16 = jnp.bfloat16
I32 = jnp.int32

D_MODEL = 2048
DEPTH = 4
NH_M = 8
DH_M = 256
CONV_K = 4
CHUNK = 128
DH_A = 128
NH_A = 16
NKV_A = 4
GRP_A = NH_A // NKV_A
NH_I = 16
DI = 64
TOPK_MAX = 256
QBLK = 128
KBLK = 512
N_BUCKETS = 32
MAX_DIST = 128
N_GROUPS = 4
EXP_PER_GROUP = 8
N_EXPERTS = 32
TOP_K = 2
D_FF_E = 512
MOE_BLK = 128
EPS = 1e-6

C_Q, C_K, C_V, C_O = 0, 2048, 4096, 6144
C_QA, C_KA, C_VA, C_QI, C_GM, C_GA = 8192, 10240, 10752, 11264, 12288, 14336
N_BIG = 16384
S_KI, S_WI, S_IP, S_FP = 0, 64, 80, 88

INT_MIN = -(2 ** 31)
NEG = -1e30
VMEM_LIMIT = 56 * 1024 * 1024


def _cparams(sem):
    return pltpu.CompilerParams(dimension_semantics=sem, vmem_limit_bytes=VMEM_LIMIT)


def _split(a):
    hi = a.astype(BF16)
    lo = (a - hi.astype(F32)).astype(BF16)
    return hi, lo


def _dot(a, b):
    return jnp.dot(a, b, preferred_element_type=F32)


def _dot3(a, b):
    ah, al = _split(a)
    bh, bl = _split(b)
    return _dot(ah, bh) + (_dot(al, bh) + _dot(ah, bl))


def _dot_nt(a, b):
    return lax.dot_general(a, b, (((1,), (1,)), ((), ())), preferred_element_type=F32)


def _dot_tn(a, b):
    return lax.dot_general(a, b, (((0,), (0,)), ((), ())), preferred_element_type=F32)


def _sigmoid(x):
    return 1.0 / (1.0 + jnp.exp(-x))


def _ada_kernel(c_ref, w_ref, b_ref, o_ref):
    c = c_ref[...]
    ca = c * _sigmoid(c)
    o_ref[0] = _dot3(ca, w_ref[0]) + b_ref[0]


def _ada_call(c16, w_ada, b_ada3):
    tn = 1024
    n6 = w_ada.shape[-1]
    return pl.pallas_call(
        _ada_kernel,
        grid=(DEPTH, n6 // tn),
        in_specs=[
            pl.BlockSpec((16, D_MODEL), lambda l, n: (0, 0)),
            pl.BlockSpec((1, D_MODEL, tn), lambda l, n: (l, 0, n)),
            pl.BlockSpec((1, 1, tn), lambda l, n: (l, 0, n)),
        ],
        out_specs=pl.BlockSpec((1, 16, tn), lambda l, n: (l, 0, n)),
        out_shape=jax.ShapeDtypeStruct((DEPTH, 16, n6), F32),
        compiler_params=_cparams(("arbitrary", "arbitrary")),
        name="adaln",
    )(c16, w_ada, b_ada3)


def _in_kernel(x_ref, nw_ref, sc_ref, sh_ref, wb_ref, ws_ref, z_ref, zs_ref, h_scr):
    @pl.when(pl.program_id(1) == 0)
    def _():
        x = x_ref[...]
        r = lax.rsqrt(jnp.mean(x * x, axis=-1, keepdims=True) + EPS)
        h = (x * r) * nw_ref[...] * (1.0 + sc_ref[...]) + sh_ref[...]
        h_scr[...] = h.astype(BF16)
        zs_ref[...] = _dot3(h, ws_ref[...])

    z_ref[...] = _dot(h_scr[...], wb_ref[...])


def _in_call(x2, nw, sc, sh, w_big, w_small):
    S = x2.shape[0]
    tm, tn = 1024, 512
    row = lambda m, n: (0, 0)
    return pl.pallas_call(
        _in_kernel,
        grid=(S // tm, N_BIG // tn),
        in_specs=[
            pl.BlockSpec((tm, D_MODEL), lambda m, n: (m, 0)),
            pl.BlockSpec((1, D_MODEL), row),
            pl.BlockSpec((1, D_MODEL), row),
            pl.BlockSpec((1, D_MODEL), row),
            pl.BlockSpec((D_MODEL, tn), lambda m, n: (0, n)),
            pl.BlockSpec((D_MODEL, 128), row),
        ],
        out_specs=[
            pl.BlockSpec((tm, tn), lambda m, n: (m, n)),
            pl.BlockSpec((tm, 128), lambda m, n: (m, 0)),
        ],
        out_shape=[
            jax.ShapeDtypeStruct((S, N_BIG), F32),
            jax.ShapeDtypeStruct((S, 128), F32),
        ],
        scratch_shapes=[pltpu.VMEM((tm, D_MODEL), BF16)],
        compiler_params=_cparams(("arbitrary", "arbitrary")),
        name="in_proj",
    )(x2, nw, sc, sh, w_big, w_small)


def _mlstm_kernel(q_ref, k_ref, v_ref, o_ref, zs_ref, cw_ref, bg_ref, nw_ref, y_ref,
                  qprev, kprev, qs, ks, c_st, n_st, m_st):
    L = CHUNK

    @pl.when(pl.program_id(0) == 0)
    def _():
        qprev[...] = jnp.zeros_like(qprev)
        kprev[...] = jnp.zeros_like(kprev)
        c_st[...] = jnp.zeros_like(c_st)
        n_st[...] = jnp.zeros_like(n_st)
        m_st[...] = jnp.zeros_like(m_st)

    row = lax.broadcasted_iota(I32, (L, NH_M * DH_M), 0)

    def conv_silu(cur_ref, prev_ref, woff):
        cur = cur_ref[...]
        prev = prev_ref[...]
        out = None
        for j in range(CONV_K):
            s = CONV_K - 1 - j
            if s == 0:
                u = cur
            else:
                u = jnp.where(row < s, pltpu.roll(prev, s, 0), pltpu.roll(cur, s, 0))
            term = u * cw_ref[j:j + 1, woff:woff + NH_M * DH_M]
            out = term if out is None else out + term
        prev_ref[...] = cur
        return out * _sigmoid(out)

    qs[...] = conv_silu(q_ref, qprev, 0)
    ks[...] = conv_silu(k_ref, kprev, NH_M * DH_M) * (DH_M ** -0.5)

    G = zs_ref[...] + bg_ref[...]
    LF = jnp.minimum(G, 0.0) - jnp.log1p(jnp.exp(-jnp.abs(G)))
    ri = lax.broadcasted_iota(I32, (L, L), 0)
    ci = lax.broadcasted_iota(I32, (L, L), 1)
    causal = ci <= ri
    tril = jnp.where(causal, 1.0, 0.0).astype(BF16)
    lf_hi, lf_lo = _split(LF)
    lf_lo2 = (LF - lf_hi.astype(F32) - lf_lo.astype(F32)).astype(BF16)
    B = _dot(tril, lf_hi) + (_dot(tril, lf_lo) + _dot(tril, lf_lo2))
    GT = G.T
    BT = B.T

    for h in range(NH_M):
        cs = slice(h * DH_M, (h + 1) * DH_M)
        b_col = B[:, S_FP + h:S_FP + h + 1]
        ig_col = G[:, S_IP + h:S_IP + h + 1]
        b_row = BT[S_FP + h:S_FP + h + 1, :]
        ig_row = GT[S_IP + h:S_IP + h + 1, :]
        g = B[L - 1:L, S_FP + h:S_FP + h + 1]
        m_prev = m_st[h:h + 1, 0:1]

        dmat = jnp.where(causal, b_col - b_row + ig_row, -jnp.inf)
        m_intra = jnp.max(dmat, axis=-1, keepdims=True)
        a_col = g - b_col + ig_col
        m_loc = jnp.max(a_col, axis=0, keepdims=True)
        w_loc = jnp.exp(a_col - m_loc)

        qh = qs[:, cs]
        kh = ks[:, cs]
        vh = v_ref[:, cs]
        qb = qh.astype(BF16)
        kb = kh.astype(BF16)
        vb = vh.astype(BF16)
        c_prev = c_st[h]
        n_prev = n_st[h:h + 1, :]

        inter = b_col + m_prev
        m_t = jnp.maximum(inter, m_intra)
        s_inter = jnp.exp(inter - m_t)
        w_intra = jnp.exp(dmat - m_t) * _dot_nt(qb, kb)
        num = _dot(w_intra.astype(BF16), vb) + s_inter * _dot(qb, c_prev.astype(BF16))
        den = (jnp.sum(w_intra, axis=-1, keepdims=True)
               + s_inter * jnp.sum(qh * n_prev, axis=-1, keepdims=True))
        hh = num / jnp.maximum(jnp.abs(den), jnp.exp(-m_t))
        hn = hh * lax.rsqrt(jnp.mean(hh * hh, axis=-1, keepdims=True) + EPS)
        y_ref[:, cs] = (hn * nw_ref[:, cs] * _sigmoid(o_ref[:, cs])).astype(y_ref.dtype)

        kw = kh * w_loc
        c_loc = _dot_tn(kw.astype(BF16), vb)
        n_loc = jnp.sum(kw, axis=0, keepdims=True)
        m_new = jnp.maximum(g + m_prev, m_loc)
        s_p = jnp.exp(g + m_prev - m_new)
        s_l = jnp.exp(m_loc - m_new)
        c_st[h] = s_p * c_prev + s_l * c_loc
        n_st[h:h + 1, :] = s_p * n_prev + s_l * n_loc
        m_st[h:h + 1, :] = jnp.broadcast_to(m_new, (1, 128))


def _mlstm_call(z, zs, conv_w, bg_row, nw_row):
    S = z.shape[0]
    W = NH_M * DH_M
    L = CHUNK
    row = lambda c: (0, 0)
    return pl.pallas_call(
        _mlstm_kernel,
        grid=(S // L,),
        in_specs=[
            pl.BlockSpec((L, W), lambda c: (c, C_Q // W)),
            pl.BlockSpec((L, W), lambda c: (c, C_K // W)),
            pl.BlockSpec((L, W), lambda c: (c, C_V // W)),
            pl.BlockSpec((L, W), lambda c: (c, C_O // W)),
            pl.BlockSpec((L, 128), lambda c: (c, 0)),
            pl.BlockSpec((CONV_K, 2 * W), row),
            pl.BlockSpec((1, 128), row),
            pl.BlockSpec((1, W), row),
        ],
        out_specs=pl.BlockSpec((L, W), lambda c: (c, 0)),
        out_shape=jax.ShapeDtypeStruct((S, W), BF16),
        scratch_shapes=[
            pltpu.VMEM((L, W), F32), pltpu.VMEM((L, W), F32),
            pltpu.VMEM((L, W), F32), pltpu.VMEM((L, W), F32),
            pltpu.VMEM((NH_M, DH_M, DH_M), F32),
            pltpu.VMEM((NH_M, DH_M), F32),
            pltpu.VMEM((NH_M, 128), F32),
        ],
        compiler_params=_cparams(("arbitrary",)),
        name="mlstm",
    )(z, z, z, z, zs, conv_w, bg_row, nw_row)


def _kvprep_kernel(ka_ref, va_ref, zs_ref, kn_ref, kan_ref, vab_ref, kie_ref, kio_ref):
    for n in range(NKV_A):
        cs = slice(n * DH_A, (n + 1) * DH_A)
        k = ka_ref[:, cs]
        r = lax.rsqrt(jnp.mean(k * k, axis=-1, keepdims=True) + EPS)
        kan_ref[:, cs] = ((k * r) * kn_ref[...]).astype(BF16)
    vab_ref[...] = va_ref[...].astype(BF16)
    zs = zs_ref[...]
    lane = lax.broadcasted_iota(I32, zs.shape, 1)
    kie_ref[...] = jnp.where(lane < DI, zs, 0.0).astype(BF16)
    kio_ref[...] = jnp.where(lane >= DI, pltpu.roll(zs, DI, 1), 0.0).astype(BF16)


def _kvprep_call(z, zs, kn_row):
    S = z.shape[0]
    tm = 512
    W = NKV_A * DH_A
    return pl.pallas_call(
        _kvprep_kernel,
        grid=(S // tm,),
        in_specs=[
            pl.BlockSpec((tm, W), lambda m: (m, C_KA // W)),
            pl.BlockSpec((tm, W), lambda m: (m, C_VA // W)),
            pl.BlockSpec((tm, 128), lambda m: (m, 0)),
            pl.BlockSpec((1, DH_A), lambda m: (0, 0)),
        ],
        out_specs=[
            pl.BlockSpec((tm, W), lambda m: (m, 0)),
            pl.BlockSpec((tm, W), lambda m: (m, 0)),
            pl.BlockSpec((tm, 128), lambda m: (m, 0)),
            pl.BlockSpec((tm, 128), lambda m: (m, 0)),
        ],
        out_shape=[
            jax.ShapeDtypeStruct((S, W), BF16),
            jax.ShapeDtypeStruct((S, W), BF16),
            jax.ShapeDtypeStruct((S, 128), BF16),
            jax.ShapeDtypeStruct((S, 128), BF16),
        ],
        compiler_params=_cparams(("arbitrary",)),
        name="kv_prep",
    )(z, z, zs, kn_row)


def _sortable(x):
    b = pltpu.bitcast(x, I32)
    return jnp.where(b < 0, b ^ jnp.int32(0x7FFFFFFF), b)


def _dsa_kernel(topk, qi_ref, qa_ref, zs_ref, kie_ref, kio_ref, kan_ref, vab_ref, qn_ref,
                tz_ref, bfar_ref, y_ref,
                skey, qst, qist, wbc, thr_s, jcut_s, m_s, l_s, acc_s):
    i = pl.program_id(0)
    S = skey.shape[1]
    nkb = i // (KBLK // QBLK) + 1
    nfar = jnp.maximum((i - 1) // (KBLK // QBLK), 0)
    lane128 = lax.broadcasted_iota(I32, (QBLK, 128), 1)
    rowq = lax.broadcasted_iota(I32, (QBLK, KBLK), 0) + i * QBLK
    lanek = lax.broadcasted_iota(I32, (QBLK, KBLK), 1)

    for p in range(NH_I // 2):
        qist[p * QBLK:(p + 1) * QBLK, :] = qi_ref[:, p * 128:(p + 1) * 128].astype(BF16)
    zs = zs_ref[...]
    for h in range(NH_I):
        wcol = zs[:, S_WI + h:S_WI + h + 1] * ((NH_I * DI) ** -0.5)
        wbc[h] = jnp.broadcast_to(wcol, (QBLK, 128))
    for h in range(NH_A):
        q = qa_ref[:, h * DH_A:(h + 1) * DH_A]
        r = lax.rsqrt(jnp.mean(q * q, axis=-1, keepdims=True) + EPS)
        qst[h * QBLK:(h + 1) * QBLK, :] = ((q * r) * qn_ref[...] * (DH_A ** -0.5)).astype(BF16)

    def score_block(jb, carry):
        k0 = pl.multiple_of(jb * KBLK, KBLK)
        se = _dot_nt(qist[...], kie_ref[pl.ds(k0, KBLK), :])
        so = _dot_nt(qist[...], kio_ref[pl.ds(k0, KBLK), :])
        cols = []
        for c in range(KBLK // 128):
            acc = None
            for p in range(NH_I // 2):
                te = jnp.maximum(se[p * QBLK:(p + 1) * QBLK, c * 128:(c + 1) * 128], 0.0) * wbc[2 * p]
                to = jnp.maximum(so[p * QBLK:(p + 1) * QBLK, c * 128:(c + 1) * 128], 0.0) * wbc[2 * p + 1]
                acc = te + to if acc is None else acc + te + to
            cols.append(acc)
        sc = jnp.concatenate(cols, axis=1)
        sc = jnp.where(sc == 0.0, 0.0, sc)
        valid = (lanek + k0) <= rowq
        skey[:, pl.ds(k0, KBLK)] = jnp.where(valid, _sortable(sc), jnp.int32(INT_MIN))
        return carry

    lax.fori_loop(0, nkb, score_block, 0)

    def count_rows(pred_fn):
        def body(jb, cnt):
            k0 = pl.multiple_of(jb * KBLK, KBLK)
            for c in range(KBLK // 128):
                tile = skey[:, pl.ds(k0 + c * 128, 128)]
                cnt = cnt + jnp.where(pred_fn(tile, k0 + c * 128), 1, 0)
            return cnt
        cnt = lax.fori_loop(0, nkb, body, jnp.zeros((QBLK, 128), I32))
        return jnp.broadcast_to(jnp.sum(cnt, axis=1, keepdims=True), (QBLK, 128))

    def bit_pass(b, cand):
        trial = cand ^ jnp.left_shift(jnp.int32(1), 31 - b)
        cnt = count_rows(lambda t, _: t >= trial)
        return jnp.where(cnt >= topk, trial, cand)

    thr = lax.fori_loop(0, 32, bit_pass, jnp.full((QBLK, 128), INT_MIN, I32))
    c_gt = count_rows(lambda t, _: t > thr)
    c_eq = count_rows(lambda t, _: t == thr)
    need = topk - c_gt
    short = thr == INT_MIN
    thr_s[...] = jnp.where(short, jnp.int32(INT_MIN + 1), thr)
    jcut_s[...] = jnp.full((QBLK, 128), S, I32)
    tie = jnp.logical_and(jnp.logical_not(short), c_eq > need)

    @pl.when(jnp.max(jnp.where(tie, 1, 0)) > 0)
    def _():
        def idx_pass(b, pos):
            t = pos + jnp.left_shift(jnp.int32(1), 13 - b)
            cnt = count_rows(lambda tl, c0: jnp.logical_and(tl == thr, (lane128 + c0) < t))
            return jnp.where(cnt < need, t, pos)
        pos = lax.fori_loop(0, 14, idx_pass, jnp.zeros((QBLK, 128), I32))
        jcut_s[...] = jnp.where(tie, pos, S)

    m_s[...] = jnp.full(m_s.shape, NEG, F32)
    l_s[...] = jnp.zeros_like(l_s)
    acc_s[...] = jnp.zeros_like(acc_s)
    RG = GRP_A * QBLK

    def attend(jb, near):
        k0 = pl.multiple_of(jb * KBLK, KBLK)
        thr_c = thr_s[:, 0:1]
        jc_c = jcut_s[:, 0:1]
        sk = skey[:, pl.ds(k0, KBLK)]
        sel = jnp.logical_or(sk > thr_c,
                             jnp.logical_and(sk == thr_c, (lanek + k0) <= jc_c))
        negm = jnp.where(sel, 0.0, NEG)
        if near:
            off = pl.multiple_of(KBLK - (i * QBLK - k0), 128)
        for n in range(NKV_A):
            kn = kan_ref[pl.ds(k0, KBLK), n * DH_A:(n + 1) * DH_A]
            vn = vab_ref[pl.ds(k0, KBLK), n * DH_A:(n + 1) * DH_A]
            lg = _dot_nt(qst[n * RG:(n + 1) * RG, :], kn)
            ps = []
            for g in range(GRP_A):
                h = n * GRP_A + g
                rs = slice(h * QBLK, (h + 1) * QBLK)
                x = lg[g * QBLK:(g + 1) * QBLK, :] + negm
                if near:
                    x = x + tz_ref[h, :, pl.ds(off, KBLK)]
                else:
                    x = x + bfar_ref[h:h + 1, 0:1]
                m_old = m_s[rs, :]
                m_new = jnp.maximum(m_old, jnp.max(x, axis=-1, keepdims=True))
                p = jnp.exp(x - m_new[:, 0:1])
                alpha = jnp.exp(m_old - m_new)
                l_s[rs, :] = alpha * l_s[rs, :] + jnp.sum(p, axis=-1, keepdims=True)
                m_s[rs, :] = m_new
                acc_s[rs, :] = alpha * acc_s[rs, :]
                ps.append(p.astype(BF16))
            pv = _dot(jnp.concatenate(ps, axis=0), vn)
            acc_s[n * RG:(n + 1) * RG, :] += pv

    def far_body(jb, carry):
        attend(jb, False)
        return carry

    def near_body(jb, carry):
        attend(jb, True)
        return carry

    lax.fori_loop(0, nfar, far_body, 0)
    lax.fori_loop(nfar, nkb, near_body, 0)

    for h in range(NH_A):
        rs = slice(h * QBLK, (h + 1) * QBLK)
        y_ref[:, h * DH_A:(h + 1) * DH_A] = (acc_s[rs, :] / l_s[rs, :]).astype(y_ref.dtype)


def _dsa_call(z, zs, kie, kio, kan, vab, qn_row, tz, bfar):
    S = z.shape[0]
    topk = min(TOPK_MAX, S // 4)
    WQ = NH_A * DH_A
    WI = NH_I * DI
    res = dict(pipeline_mode=pl.Buffered(1))
    return pl.pallas_call(
        functools.partial(_dsa_kernel, topk),
        grid=(S // QBLK,),
        in_specs=[
            pl.BlockSpec((QBLK, WI), lambda i: (i, C_QI // WI)),
            pl.BlockSpec((QBLK, WQ), lambda i: (i, C_QA // WQ)),
            pl.BlockSpec((QBLK, 128), lambda i: (i, 0)),
            pl.BlockSpec((S, 128), lambda i: (0, 0), **res),
            pl.BlockSpec((S, 128), lambda i: (0, 0), **res),
            pl.BlockSpec((S, NKV_A * DH_A), lambda i: (0, 0), **res),
            pl.BlockSpec((S, NKV_A * DH_A), lambda i: (0, 0), **res),
            pl.BlockSpec((1, DH_A), lambda i: (0, 0)),
            pl.BlockSpec((NH_A, QBLK, 2 * KBLK), lambda i: (0, 0, 0), **res),
            pl.BlockSpec((NH_A, 128), lambda i: (0, 0)),
        ],
        out_specs=pl.BlockSpec((QBLK, WQ), lambda i: (i, 0)),
        out_shape=jax.ShapeDtypeStruct((S, WQ), BF16),
        scratch_shapes=[
            pltpu.VMEM((QBLK, S), I32),
            pltpu.VMEM((NH_A * QBLK, DH_A), BF16),
            pltpu.VMEM((NH_I // 2 * QBLK, 128), BF16),
            pltpu.VMEM((NH_I, QBLK, 128), F32),
            pltpu.VMEM((QBLK, 128), I32),
            pltpu.VMEM((QBLK, 128), I32),
            pltpu.VMEM((NH_A * QBLK, 128), F32),
            pltpu.VMEM((NH_A * QBLK, 128), F32),
            pltpu.VMEM((NH_A * QBLK, DH_A), F32),
        ],
        compiler_params=_cparams(("arbitrary",)),
        name="dsa",
    )(z, z, zs, kie, kio, kan, vab, qn_row, tz, bfar)


def _t5_bucket(dist):
    n = jnp.maximum(dist, 0)
    max_exact = N_BUCKETS // 2
    nf = jnp.maximum(n, max_exact).astype(F32)
    large = max_exact + (jnp.log(nf / max_exact) / math.log(MAX_DIST / max_exact)
                         * (N_BUCKETS - max_exact)).astype(I32)
    large = jnp.minimum(large, N_BUCKETS - 1)
    return jnp.where(n < max_exact, n, large)


def _bias_tables(rel_bias):
    qq = jnp.arange(QBLK, dtype=I32)[:, None]
    cc = jnp.arange(2 * KBLK, dtype=I32)[None, :]
    bucket = _t5_bucket(qq - cc + KBLK)
    tz = jnp.transpose(rel_bias[bucket], (2, 0, 1)).astype(F32)
    bfar = jnp.broadcast_to(rel_bias[N_BUCKETS - 1][:, None], (NH_A, 128)).astype(F32)
    return tz, bfar


def _merge_kernel(ym_ref, ya_ref, wm_ref, wa_ref, gm_ref, ga_ref, o_ref):
    a = _dot(ym_ref[...], wm_ref[0])
    b = _dot(ya_ref[...], wa_ref[0])
    o_ref[...] = (_sigmoid(gm_ref[...]) * a + _sigmoid(ga_ref[...]) * b).astype(o_ref.dtype)


def _merge_call(l, ym, ya, wpm, wpa, z):
    S = ym.shape[0]
    tm, tn = 1024, 512
    D = D_MODEL
    return pl.pallas_call(
        _merge_kernel,
        grid=(S // tm, D // tn),
        in_specs=[
            pl.BlockSpec((tm, D), lambda m, n: (m, 0)),
            pl.BlockSpec((tm, D), lambda m, n: (m, 0)),
            pl.BlockSpec((1, D, tn), lambda m, n: (l, 0, n)),
            pl.BlockSpec((1, D, tn), lambda m, n: (l, 0, n)),
            pl.BlockSpec((tm, tn), lambda m, n: (m, C_GM // tn + n)),
            pl.BlockSpec((tm, tn), lambda m, n: (m, C_GA // tn + n)),
        ],
        out_specs=pl.BlockSpec((tm, tn), lambda m, n: (m, n)),
        out_shape=jax.ShapeDtypeStruct((S, D), BF16),
        compiler_params=_cparams(("arbitrary", "arbitrary")),
        name="merge",
    )(ym, ya, wpm, wpa, z, z)


def _out_kernel(mg_ref, wo_ref, x_ref, g1_ref, nw_ref, sc_ref, sh_ref, wr_ref, br_ref,
                xo_ref, h2_ref, lg_ref):
    xn = x_ref[...] + g1_ref[...] * _dot(mg_ref[...], wo_ref[0])
    xo_ref[...] = xn
    r = lax.rsqrt(jnp.mean(xn * xn, axis=-1, keepdims=True) + EPS)
    h = (xn * r) * nw_ref[...] * (1.0 + sc_ref[...]) + sh_ref[...]
    h2_ref[...] = h.astype(BF16)
    lg_ref[...] = _dot3(h, wr_ref[...]) + br_ref[...]


def _out_call(l, merged, wo, x2, g1, nw, sc, sh, w_router, b_router):
    S = x2.shape[0]
    tm = 256
    D = D_MODEL
    row = lambda m: (0, 0)
    return pl.pallas_call(
        _out_kernel,
        grid=(S // tm,),
        in_specs=[
            pl.BlockSpec((tm, D), lambda m: (m, 0)),
            pl.BlockSpec((1, D, D), lambda m: (l, 0, 0), pipeline_mode=pl.Buffered(1)),
            pl.BlockSpec((tm, D), lambda m: (m, 0)),
            pl.BlockSpec((1, D), row), pl.BlockSpec((1, D), row),
            pl.BlockSpec((1, D), row), pl.BlockSpec((1, D), row),
            pl.BlockSpec((D, 128), row),
            pl.BlockSpec((1, 128), row),
        ],
        out_specs=[
            pl.BlockSpec((tm, D), lambda m: (m, 0)),
            pl.BlockSpec((tm, D), lambda m: (m, 0)),
            pl.BlockSpec((tm, 128), lambda m: (m, 0)),
        ],
        out_shape=[
            jax.ShapeDtypeStruct((S, D), F32),
            jax.ShapeDtypeStruct((S, D), BF16),
            jax.ShapeDtypeStruct((S, 128), F32),
        ],
        compiler_params=_cparams(("arbitrary",)),
        name="out_proj",
    )(merged, wo, x2, g1, nw, sc, sh, w_router, b_router)


def _moe_kernel(be_ref, nu_ref, xs_ref, gate_ref, w1_ref, w3_ref, w2_ref, o_ref):
    b = pl.program_id(0)

    @pl.when(b < nu_ref[0])
    def _():
        xb = xs_ref[...]
        a = _dot(xb, w1_ref[0, 0].astype(BF16))
        c = _dot(xb, w3_ref[0, 0].astype(BF16))
        hmid = (a * _sigmoid(a)) * c
        y = _dot(hmid.astype(BF16), w2_ref[0, 0].astype(BF16))
        o_ref[...] = y * gate_ref[...]

    @pl.when(b >= nu_ref[0])
    def _():
        o_ref[...] = jnp.zeros_like(o_ref)


def _moe_call(l, blk_exp, nused, xs, row_gate, w1, w3, w2):
    rows = xs.shape[0]
    nblk = rows // MOE_BLK
    D = D_MODEL
    grid_spec = pltpu.PrefetchScalarGridSpec(
        num_scalar_prefetch=2,
        grid=(nblk,),
        in_specs=[
            pl.BlockSpec((MOE_BLK, D), lambda b, be, nu: (b, 0)),
            pl.BlockSpec((MOE_BLK, 1), lambda b, be, nu: (b, 0)),
            pl.BlockSpec((1, 1, D, D_FF_E), lambda b, be, nu: (l, be[b], 0, 0)),
            pl.BlockSpec((1, 1, D, D_FF_E), lambda b, be, nu: (l, be[b], 0, 0)),
            pl.BlockSpec((1, 1, D_FF_E, D), lambda b, be, nu: (l, be[b], 0, 0)),
        ],
        out_specs=pl.BlockSpec((MOE_BLK, D), lambda b, be, nu: (b, 0)),
    )
    return pl.pallas_call(
        _moe_kernel,
        grid_spec=grid_spec,
        out_shape=jax.ShapeDtypeStruct((rows, D), F32),
        compiler_params=_cparams(("arbitrary",)),
        name="moe_ffn",
    )(blk_exp, nused, xs, row_gate, w1, w3, w2)


def _route(logits):
    n = logits.shape[0]
    g_logit = logits[:, :N_GROUPS]
    g_sel = jnp.argmax(g_logit, axis=-1)
    p_grp = jnp.take_along_axis(jax.nn.softmax(g_logit, axis=-1), g_sel[:, None], axis=-1)
    e_logit = logits[:, N_GROUPS:N_GROUPS + N_EXPERTS].reshape(n, N_GROUPS, EXP_PER_GROUP)
    e_logit = jnp.take_along_axis(e_logit, g_sel[:, None, None], axis=1)[:, 0]
    top_v, top_i = lax.top_k(e_logit, TOP_K)
    gate = (p_grp * jax.nn.softmax(top_v, axis=-1)).reshape(-1)
    expert = (g_sel[:, None] * EXP_PER_GROUP + top_i).reshape(-1).astype(I32)
    m = n * TOP_K
    onehot = (expert[:, None] == jnp.arange(N_EXPERTS, dtype=I32)[None, :]).astype(I32)
    rank = jnp.take_along_axis(jnp.cumsum(onehot, axis=0) - onehot, expert[:, None], axis=1)[:, 0]
    counts = jnp.sum(onehot, axis=0)
    pcounts = (counts + MOE_BLK - 1) // MOE_BLK * MOE_BLK
    pends = jnp.cumsum(pcounts)
    dest = (pends - pcounts)[expert] + rank
    nblk = -(-m // MOE_BLK) + N_EXPERTS
    rows = nblk * MOE_BLK
    tok = jnp.arange(m, dtype=I32) // TOP_K
    row_tok = jnp.zeros((rows,), I32).at[dest].set(tok)
    row_gate = jnp.zeros((rows,), F32).at[dest].set(gate)
    blk_exp = jnp.minimum(jnp.searchsorted(pends, jnp.arange(nblk, dtype=I32) * MOE_BLK,
                                           side='right'), N_EXPERTS - 1).astype(I32)
    nused = (pends[-1] // MOE_BLK).astype(I32).reshape(1)
    return row_tok, row_gate, blk_exp, nused, dest.reshape(n, TOP_K)


def _layer(l, x2, mod, norm1, norm2, w_big, w_small, conv_w, bg_row, mlstm_norm, q_norm, k_norm,
           tz, bfar, wpm, wpa, wo, w_router, b_router, w1, w3, w2):
    D = D_MODEL
    sh1, sc1, g1, sh2, sc2, g2 = [mod[:, j * D:(j + 1) * D] for j in range(6)]
    z, zs = _in_call(x2, norm1[None, :], sc1, sh1, w_big, w_small)
    ym = _mlstm_call(z, zs, conv_w, bg_row, mlstm_norm[None, :])
    kan, vab, kie, kio = _kvprep_call(z, zs, k_norm[None, :])
    ya = _dsa_call(z, zs, kie, kio, kan, vab, q_norm[None, :], tz, bfar)
    merged = _merge_call(l, ym, ya, wpm, wpa, z)
    xn, h2, logits = _out_call(l, merged, wo, x2, g1, norm2[None, :], sc2, sh2, w_router, b_router)
    row_tok, row_gate, blk_exp, nused, dest = _route(logits)
    xs = h2[row_tok]
    yb = _moe_call(l, blk_exp, nused, xs, row_gate[:, None], w1, w3, w2)
    y = yb[dest[:, 0]] + yb[dest[:, 1]]
    return xn + g2 * y


def kernel(x, c, w_ada, b_ada, norm1, norm2, w_in, conv_w, b_gate, mlstm_norm, q_norm, k_norm,
           rel_bias, w_proj_m, w_proj_a, w_out, w_grp, b_grp, w_exp, b_exp, w1, w3, w2):
    B, S, D = x.shape
    assert B == 1 and D == D_MODEL
    x2 = x.reshape(S, D)
    c16 = jnp.pad(c, ((0, 16 - B), (0, 0)))
    mods = _ada_call(c16, w_ada, b_ada[:, None, :])
    tz, bfar = _bias_tables(rel_bias)
    wpm = w_proj_m.astype(BF16)
    wpa = w_proj_a.astype(BF16)
    wo = w_out.astype(BF16)
    for l in range(DEPTH):
        wl = w_in[l]
        w_big = jnp.concatenate([wl[:, 0:8192], wl[:, 8208:12304], wl[:, 12384:16480]],
                                axis=1).astype(BF16)
        w_small = jnp.concatenate([wl[:, 12304:12384], wl[:, 8192:8208],
                                   jnp.zeros((D, 32), F32)], axis=1)
        bg_row = jnp.zeros((1, 128), F32).at[0, S_IP:S_IP + 2 * NH_M].set(b_gate[l])
        w_router = jnp.concatenate([w_grp[l], w_exp[l],
                                    jnp.zeros((D, 128 - N_GROUPS - N_EXPERTS), F32)], axis=1)
        b_router = jnp.concatenate([b_grp[l], b_exp[l],
                                    jnp.zeros((128 - N_GROUPS - N_EXPERTS,), F32)])[None, :]
        x2 = _layer(l, x2, mods[l, 0:1, :], norm1[l], norm2[l], w_big, w_small, conv_w[l], bg_row,
                    mlstm_norm[l], q_norm[l], k_norm[l], tz, bfar, wpm, wpa, wo,
                    w_router, b_router, w1, w3, w2)
    return x2.reshape(B, S, D)
```

```python
import functools
import math

import numpy as np
import jax
import jax.numpy as jnp
from jax import lax
from jax.experimental import pallas as pl
from jax.experimental.pallas import tpu as pltpu

F32 = jnp.float32
BF16 = jnp.bfloat16
I32 = jnp.int32

D_MODEL = 2048
DEPTH = 4
NH_M = 8
DH_M = 256
CONV_K = 4
CHUNK = 128
DH_A = 128
NH_A = 16
NKV_A = 4
GRP_A = NH_A // NKV_A
NH_I = 16
DI = 64
TOPK_MAX = 256
QBLK = 128
KBLK = 512
N_BUCKETS = 32
MAX_DIST = 128
N_GROUPS = 4
EXP_PER_GROUP = 8
N_EXPERTS = 32
TOP_K = 2
D_FF_E = 512
MOE_BLK = 128
EPS = 1e-6

IN_SIZES = (2 * NH_M * DH_M, NH_M * DH_M, NH_M * DH_M, NH_M, NH_M, NH_A * DH_A, NKV_A * DH_A,
            NKV_A * DH_A, NH_I * DI, DI, NH_I, D_MODEL, D_MODEL)
D_IN = sum(IN_SIZES)
C_Q, C_K, C_V, C_O = 0, 2048, 4096, 6144
C_QA, C_KA, C_VA, C_QI, C_GM, C_GA = 8192, 10240, 10752, 11264, 12288, 14336
N_BIG = 16384
S_KI, S_WI, S_IP, S_FP = 0, 64, 80, 88

INT_MIN = -(2 ** 31)
NEG = -1e30
LOG2E = math.log2(math.e)
QSCALE = DH_A ** -0.5 * LOG2E
VT_ROWS = DH_A + 16
VMEM_LIMIT = 56 * 1024 * 1024


def _cparams(sem):
    return pltpu.CompilerParams(dimension_semantics=sem, vmem_limit_bytes=VMEM_LIMIT)


def _split(a):
    hi = a.astype(BF16)
    lo = (a - hi.astype(F32)).astype(BF16)
    return hi, lo


def _dot(a, b):
    return jnp.dot(a, b, preferred_element_type=F32)


def _dot3(a, b):
    ah, al = _split(a)
    bh, bl = _split(b)
    return _dot(ah, bh) + (_dot(al, bh) + _dot(ah, bl))


def _dot_nt(a, b):
    return lax.dot_general(a, b, (((1,), (1,)), ((), ())), preferred_element_type=F32)


def _dot_tn(a, b):
    return lax.dot_general(a, b, (((0,), (0,)), ((), ())), preferred_element_type=F32)


def _sigmoid(x):
    return 1.0 / (1.0 + jnp.exp(-x))


def _ada_kernel(c_ref, w_ref, b_ref, o_ref):
    c = c_ref[...]
    ca = c * _sigmoid(c)
    o_ref[0] = _dot3(ca, w_ref[0]) + b_ref[0]


def _ada_call(c16, w_ada, b_ada3):
    tn = 1024
    n6 = w_ada.shape[-1]
    return pl.pallas_call(
        _ada_kernel,
        grid=(DEPTH, n6 // tn),
        in_specs=[
            pl.BlockSpec((16, D_MODEL), lambda l, n: (0, 0)),
            pl.BlockSpec((1, D_MODEL, tn), lambda l, n: (l, 0, n)),
            pl.BlockSpec((1, 1, tn), lambda l, n: (l, 0, n)),
        ],
        out_specs=pl.BlockSpec((1, 16, tn), lambda l, n: (l, 0, n)),
        out_shape=jax.ShapeDtypeStruct((DEPTH, 16, n6), F32),
        compiler_params=_cparams(("arbitrary", "arbitrary")),
        name="adaln",
    )(c16, w_ada, b_ada3)


def _wprep_kernel(w_ref, wb_ref, ws_ref):
    x = w_ref[0]
    cuts = [0] + [int(t) for t in np.cumsum(IN_SIZES)]
    seg = lambda a, b: x[:, cuts[a]:cuts[b]]
    wb_ref[0] = jnp.concatenate([seg(0, 3), seg(5, 9), seg(11, 13)], axis=1).astype(BF16)
    ws_ref[0] = jnp.concatenate([seg(9, 11), seg(3, 5), jnp.zeros((x.shape[0], 32), F32)], axis=1)


def _wprep_call(w_in):
    tm = 128
    return pl.pallas_call(
        _wprep_kernel,
        grid=(DEPTH, D_MODEL // tm),
        in_specs=[pl.BlockSpec((1, tm, D_IN), lambda l, m: (l, m, 0))],
        out_specs=[
            pl.BlockSpec((1, tm, N_BIG), lambda l, m: (l, m, 0)),
            pl.BlockSpec((1, tm, 128), lambda l, m: (l, m, 0)),
        ],
        out_shape=[
            jax.ShapeDtypeStruct((DEPTH, D_MODEL, N_BIG), BF16),
            jax.ShapeDtypeStruct((DEPTH, D_MODEL, 128), F32),
        ],
        compiler_params=_cparams(("arbitrary", "arbitrary")),
        name="w_prep",
    )(w_in)


def _in_kernel(x_ref, nw_ref, sc_ref, sh_ref, wb_ref, ws_ref, z_ref, zs_ref, h_scr):
    @pl.when(pl.program_id(1) == 0)
    def _():
        x = x_ref[...]
        r = lax.rsqrt(jnp.mean(x * x, axis=-1, keepdims=True) + EPS)
        h = (x * r) * nw_ref[...] * (1.0 + sc_ref[...]) + sh_ref[...]
        h_scr[...] = h.astype(BF16)
        zs_ref[...] = _dot3(h, ws_ref[0])

    z_ref[...] = _dot(h_scr[...], wb_ref[0])


def _in_call(l, x2, nw, sc, sh, w_big, w_small):
    S = x2.shape[0]
    tm, tn = 1024, 512
    row = lambda m, n: (0, 0)
    return pl.pallas_call(
        _in_kernel,
        grid=(S // tm, N_BIG // tn),
        in_specs=[
            pl.BlockSpec((tm, D_MODEL), lambda m, n: (m, 0)),
            pl.BlockSpec((1, D_MODEL), row),
            pl.BlockSpec((1, D_MODEL), row),
            pl.BlockSpec((1, D_MODEL), row),
            pl.BlockSpec((1, D_MODEL, tn), lambda m, n: (l, 0, n)),
            pl.BlockSpec((1, D_MODEL, 128), lambda m, n: (l, 0, 0)),
        ],
        out_specs=[
            pl.BlockSpec((tm, tn), lambda m, n: (m, n)),
            pl.BlockSpec((tm, 128), lambda m, n: (m, 0)),
        ],
        out_shape=[
            jax.ShapeDtypeStruct((S, N_BIG), F32),
            jax.ShapeDtypeStruct((S, 128), F32),
        ],
        scratch_shapes=[pltpu.VMEM((tm, D_MODEL), BF16)],
        compiler_params=_cparams(("arbitrary", "arbitrary")),
        name="in_proj",
    )(x2, nw, sc, sh, w_big, w_small)


def _mlstm_kernel(q_ref, k_ref, v_ref, o_ref, zs_ref, cw_ref, bg_ref, nw_ref, y_ref,
                  qprev, kprev, qs, ks, c_st, n_st, m_st):
    L = CHUNK

    @pl.when(pl.program_id(0) == 0)
    def _():
        qprev[...] = jnp.zeros_like(qprev)
        kprev[...] = jnp.zeros_like(kprev)
        c_st[...] = jnp.zeros_like(c_st)
        n_st[...] = jnp.zeros_like(n_st)
        m_st[...] = jnp.zeros_like(m_st)

    row = lax.broadcasted_iota(I32, (L, NH_M * DH_M), 0)

    def conv_silu(cur_ref, prev_ref, woff):
        cur = cur_ref[...]
        prev = prev_ref[...]
        out = None
        for j in range(CONV_K):
            s = CONV_K - 1 - j
            if s == 0:
                u = cur
            else:
                u = jnp.where(row < s, pltpu.roll(prev, s, 0), pltpu.roll(cur, s, 0))
            term = u * cw_ref[j:j + 1, woff:woff + NH_M * DH_M]
            out = term if out is None else out + term
        prev_ref[...] = cur
        return out * _sigmoid(out)

    qs[...] = conv_silu(q_ref, qprev, 0)
    ks[...] = conv_silu(k_ref, kprev, NH_M * DH_M) * (DH_M ** -0.5)

    G = zs_ref[...] + bg_ref[...]
    LF = jnp.minimum(G, 0.0) - jnp.log1p(jnp.exp(-jnp.abs(G)))
    ri = lax.broadcasted_iota(I32, (L, L), 0)
    ci = lax.broadcasted_iota(I32, (L, L), 1)
    causal = ci <= ri
    tril = jnp.where(causal, 1.0, 0.0).astype(BF16)
    lf_hi, lf_lo = _split(LF)
    lf_lo2 = (LF - lf_hi.astype(F32) - lf_lo.astype(F32)).astype(BF16)
    B = _dot(tril, lf_hi) + (_dot(tril, lf_lo) + _dot(tril, lf_lo2))
    GT = G.T
    BT = B.T

    for h in range(NH_M):
        cs = slice(h * DH_M, (h + 1) * DH_M)
        b_col = B[:, S_FP + h:S_FP + h + 1]
        ig_col = G[:, S_IP + h:S_IP + h + 1]
        b_row = BT[S_FP + h:S_FP + h + 1, :]
        ig_row = GT[S_IP + h:S_IP + h + 1, :]
        g = B[L - 1:L, S_FP + h:S_FP + h + 1]
        m_prev = m_st[h:h + 1, 0:1]

        dmat = jnp.where(causal, b_col - b_row + ig_row, -jnp.inf)
        m_intra = jnp.max(dmat, axis=-1, keepdims=True)
        a_col = g - b_col + ig_col
        m_loc = jnp.max(a_col, axis=0, keepdims=True)
        w_loc = jnp.exp(a_col - m_loc)

        qh = qs[:, cs]
        kh = ks[:, cs]
        vh = v_ref[:, cs]
        qb = qh.astype(BF16)
        kb = kh.astype(BF16)
        vb = vh.astype(BF16)
        c_prev = c_st[h]
        n_prev = n_st[h:h + 1, :]

        inter = b_col + m_prev
        m_t = jnp.maximum(inter, m_intra)
        s_inter = jnp.exp(inter - m_t)
        w_intra = jnp.exp(dmat - m_t) * _dot_nt(qb, kb)
        num = _dot(w_intra.astype(BF16), vb) + s_inter * _dot(qb, c_prev.astype(BF16))
        den = (jnp.sum(w_intra, axis=-1, keepdims=True)
               + s_inter * jnp.sum(qh * n_prev, axis=-1, keepdims=True))
        hh = num / jnp.maximum(jnp.abs(den), jnp.exp(-m_t))
        hn = hh * lax.rsqrt(jnp.mean(hh * hh, axis=-1, keepdims=True) + EPS)
        y_ref[:, cs] = (hn * nw_ref[:, cs] * _sigmoid(o_ref[:, cs])).astype(y_ref.dtype)

        kw = kh * w_loc
        c_loc = _dot_tn(kw.astype(BF16), vb)
        n_loc = jnp.sum(kw, axis=0, keepdims=True)
        m_new = jnp.maximum(g + m_prev, m_loc)
        s_p = jnp.exp(g + m_prev - m_new)
        s_l = jnp.exp(m_loc - m_new)
        c_st[h] = s_p * c_prev + s_l * c_loc
        n_st[h:h + 1, :] = s_p * n_prev + s_l * n_loc
        m_st[h:h + 1, :] = jnp.broadcast_to(m_new, (1, 128))


def _mlstm_call(z, zs, conv_w, bg_row, nw_row):
    S = z.shape[0]
    W = NH_M * DH_M
    L = CHUNK
    row = lambda c: (0, 0)
    return pl.pallas_call(
        _mlstm_kernel,
        grid=(S // L,),
        in_specs=[
            pl.BlockSpec((L, W), lambda c: (c, C_Q // W)),
            pl.BlockSpec((L, W), lambda c: (c, C_K // W)),
            pl.BlockSpec((L, W), lambda c: (c, C_V // W)),
            pl.BlockSpec((L, W), lambda c: (c, C_O // W)),
            pl.BlockSpec((L, 128), lambda c: (c, 0)),
            pl.BlockSpec((CONV_K, 2 * W), row),
            pl.BlockSpec((1, 128), row),
            pl.BlockSpec((1, W), row),
        ],
        out_specs=pl.BlockSpec((L, W), lambda c: (c, 0)),
        out_shape=jax.ShapeDtypeStruct((S, W), BF16),
        scratch_shapes=[
            pltpu.VMEM((L, W), F32), pltpu.VMEM((L, W), F32),
            pltpu.VMEM((L, W), F32), pltpu.VMEM((L, W), F32),
            pltpu.VMEM((NH_M, DH_M, DH_M), F32),
            pltpu.VMEM((NH_M, DH_M), F32),
            pltpu.VMEM((NH_M, 128), F32),
        ],
        compiler_params=_cparams(("arbitrary",)),
        name="mlstm",
    )(z, z, z, z, zs, conv_w, bg_row, nw_row)


def _kvprep_kernel(ka_ref, va_ref, zs_ref, kn_ref, kan_ref, vta_ref, kie_ref, kio_ref):
    tm = ka_ref.shape[0]
    for n in range(NKV_A):
        cs = slice(n * DH_A, (n + 1) * DH_A)
        k = ka_ref[:, cs]
        r = lax.rsqrt(jnp.mean(k * k, axis=-1, keepdims=True) + EPS)
        kan_ref[:, cs] = ((k * r) * kn_ref[...]).astype(BF16)
        vta_ref[n * VT_ROWS:n * VT_ROWS + DH_A, :] = va_ref[:, cs].T.astype(BF16)
        vta_ref[n * VT_ROWS + DH_A:(n + 1) * VT_ROWS, :] = jnp.ones((VT_ROWS - DH_A, tm), BF16)
    zs = zs_ref[...]
    lane = lax.broadcasted_iota(I32, zs.shape, 1)
    kie_ref[...] = jnp.where(lane < DI, zs, 0.0).astype(BF16)
    kio_ref[...] = jnp.where(lane >= DI, pltpu.roll(zs, DI, 1), 0.0).astype(BF16)


def _kvprep_call(z, zs, kn_row):
    S = z.shape[0]
    tm = 512
    W = NKV_A * DH_A
    return pl.pallas_call(
        _kvprep_kernel,
        grid=(S // tm,),
        in_specs=[
            pl.BlockSpec((tm, W), lambda m: (m, C_KA // W)),
            pl.BlockSpec((tm, W), lambda m: (m, C_VA // W)),
            pl.BlockSpec((tm, 128), lambda m: (m, 0)),
            pl.BlockSpec((1, DH_A), lambda m: (0, 0)),
        ],
        out_specs=[
            pl.BlockSpec((tm, W), lambda m: (m, 0)),
            pl.BlockSpec((NKV_A * VT_ROWS, tm), lambda m: (0, m)),
            pl.BlockSpec((tm, 128), lambda m: (m, 0)),
            pl.BlockSpec((tm, 128), lambda m: (m, 0)),
        ],
        out_shape=[
            jax.ShapeDtypeStruct((S, W), BF16),
            jax.ShapeDtypeStruct((NKV_A * VT_ROWS, S), BF16),
            jax.ShapeDtypeStruct((S, 128), BF16),
            jax.ShapeDtypeStruct((S, 128), BF16),
        ],
        compiler_params=_cparams(("arbitrary",)),
        name="kv_prep",
    )(z, z, zs, kn_row)


def _sortable(x):
    b = pltpu.bitcast(x, I32)
    return jnp.where(b < 0, b ^ jnp.int32(0x7FFFFFFF), b)


def _dsa_kernel(topk, qi_ref, qa_ref, zs_ref, kie_ref, kio_ref, kan_ref, vta_ref, qn_ref,
                tz_ref, bfar_ref, y_ref,
                skey, qst, qist, thr_s, jcut_s, m_s, acc_s):
    i = pl.program_id(0)
    S = skey.shape[0]
    R = KBLK // QBLK
    nkb = i // R + 1
    nfar = jnp.maximum((i - 1) // R, 0)
    keyi = lax.broadcasted_iota(I32, (KBLK, QBLK), 0)
    qcol = lax.broadcasted_iota(I32, (KBLK, QBLK), 1) + i * QBLK

    for p in range(NH_I // 2):
        qist[p * QBLK:(p + 1) * QBLK, :] = qi_ref[:, p * 128:(p + 1) * 128].astype(BF16)
    wT = (zs_ref[...] * ((NH_I * DI) ** -0.5)).T
    for h in range(NH_A):
        q = qa_ref[:, h * DH_A:(h + 1) * DH_A]
        r = lax.rsqrt(jnp.mean(q * q, axis=-1, keepdims=True) + EPS)
        qst[h * QBLK:(h + 1) * QBLK, :] = ((q * r) * qn_ref[...] * QSCALE).astype(BF16)

    def score_block(jb, carry):
        k0 = pl.multiple_of(jb * KBLK, KBLK)
        se = _dot_nt(kie_ref[pl.ds(k0, KBLK), :], qist[...])
        so = _dot_nt(kio_ref[pl.ds(k0, KBLK), :], qist[...])
        sc = None
        for p in range(NH_I // 2):
            te = jnp.maximum(se[:, p * QBLK:(p + 1) * QBLK], 0.0) * wT[S_WI + 2 * p:S_WI + 2 * p + 1, :]
            to = jnp.maximum(so[:, p * QBLK:(p + 1) * QBLK], 0.0) * wT[S_WI + 2 * p + 1:S_WI + 2 * p + 2, :]
            sc = te + to if sc is None else sc + te + to
        sc = jnp.where(sc == 0.0, 0.0, sc)
        valid = (keyi + k0) <= qcol
        skey[pl.ds(k0, KBLK), :] = jnp.where(valid, _sortable(sc), jnp.int32(INT_MIN))
        return carry

    lax.fori_loop(0, nkb, score_block, 0)

    def count(pred_fn):
        def body(jb, cnt):
            k0 = pl.multiple_of(jb * KBLK, KBLK)
            hit = jnp.where(pred_fn(skey[pl.ds(k0, KBLK), :], k0), 1, 0)
            return cnt + jnp.sum(hit.reshape(KBLK // 8, 8, QBLK), axis=0)
        cnt = lax.fori_loop(0, nkb, body, jnp.zeros((8, QBLK), I32))
        return jnp.sum(cnt, axis=0, keepdims=True)

    def bit_pass(b, cand):
        trial = cand ^ jnp.left_shift(jnp.int32(1), 31 - b)
        cnt = count(lambda t, _: t >= trial)
        return jnp.where(cnt >= topk, trial, cand)

    thr = lax.fori_loop(0, 32, bit_pass, jnp.full((1, QBLK), INT_MIN, I32))
    c_gt = count(lambda t, _: t > thr)
    c_eq = count(lambda t, _: t == thr)
    need = topk - c_gt
    short = thr == INT_MIN
    thr_s[...] = jnp.broadcast_to(jnp.where(short, jnp.int32(INT_MIN + 1), thr), (8, QBLK))
    jcut_s[...] = jnp.full((8, QBLK), S, I32)
    tie = jnp.logical_and(jnp.logical_not(short), c_eq > need)

    @pl.when(jnp.max(jnp.where(tie, 1, 0)) > 0)
    def _():
        def idx_pass(b, pos):
            t = pos + jnp.left_shift(jnp.int32(1), 13 - b)
            cnt = count(lambda tl, k0: jnp.logical_and(tl == thr, (keyi + k0) < t))
            return jnp.where(cnt < need, t, pos)
        pos = lax.fori_loop(0, 14, idx_pass, jnp.zeros((1, QBLK), I32))
        jcut_s[...] = jnp.broadcast_to(jnp.where(tie, pos, S), (8, QBLK))

    m_s[...] = jnp.full(m_s.shape, NEG, F32)
    acc_s[...] = jnp.zeros_like(acc_s)

    def attend(jb, near):
        k0 = pl.multiple_of(jb * KBLK, KBLK)
        thr_r = thr_s[0:1, :]
        jc_r = jcut_s[0:1, :]
        sk = skey[pl.ds(k0, KBLK), :]
        sel = jnp.logical_or(sk > thr_r,
                             jnp.logical_and(sk == thr_r, (keyi + k0) <= jc_r))
        negm = jnp.where(sel, 0.0, NEG)
        if near:
            off = pl.multiple_of(KBLK - (i * QBLK - k0), QBLK)
        for n in range(NKV_A):
            kn = kan_ref[pl.ds(k0, KBLK), n * DH_A:(n + 1) * DH_A]
            lg = _dot_nt(kn, qst[n * GRP_A * QBLK:(n + 1) * GRP_A * QBLK, :])
            ps, alphas = [], []
            for g in range(GRP_A):
                h = n * GRP_A + g
                x = lg[:, g * QBLK:(g + 1) * QBLK] + negm
                m_old = m_s[h:h + 1, :]
                if near:
                    x = x + tz_ref[h, pl.ds(off, KBLK), :]
                    m_new = jnp.maximum(m_old, jnp.max(x, axis=0, keepdims=True))
                    p = jnp.exp2(x - m_new)
                else:
                    bf = bfar_ref[h:h + 1, :]
                    m_new = jnp.maximum(m_old, jnp.max(x, axis=0, keepdims=True) + bf)
                    p = jnp.exp2(x - (m_new - bf))
                m_s[h:h + 1, :] = m_new
                alphas.append(jnp.exp2(m_old - m_new))
                ps.append(p.astype(BF16))
            pt = jnp.concatenate(ps, axis=1)
            pv = _dot(vta_ref[n * VT_ROWS:(n + 1) * VT_ROWS, pl.ds(k0, KBLK)], pt)
            acc_s[n] = jnp.concatenate(alphas, axis=1) * acc_s[n] + pv

    def far_body(jb, carry):
        attend(jb, False)
        return carry

    def near_body(jb, carry):
        attend(jb, True)
        return carry

    lax.fori_loop(0, nfar, far_body, 0)
    lax.fori_loop(nfar, nkb, near_body, 0)

    for h in range(NH_A):
        n, g = divmod(h, GRP_A)
        num = acc_s[n, 0:DH_A, g * QBLK:(g + 1) * QBLK]
        den = acc_s[n, DH_A:DH_A + 1, g * QBLK:(g + 1) * QBLK]
        y_ref[:, h * DH_A:(h + 1) * DH_A] = (num / den).T.astype(y_ref.dtype)


def _dsa_call(z, zs, kie, kio, kan, vta, qn_row, tz, bfar):
    S = z.shape[0]
    topk = min(TOPK_MAX, S // 4)
    WQ = NH_A * DH_A
    WI = NH_I * DI
    res = dict(pipeline_mode=pl.Buffered(1))
    return pl.pallas_call(
        functools.partial(_dsa_kernel, topk),
        grid=(S // QBLK,),
        in_specs=[
            pl.BlockSpec((QBLK, WI), lambda i: (i, C_QI // WI)),
            pl.BlockSpec((QBLK, WQ), lambda i: (i, C_QA // WQ)),
            pl.BlockSpec((QBLK, 128), lambda i: (i, 0)),
            pl.BlockSpec((S, 128), lambda i: (0, 0), **res),
            pl.BlockSpec((S, 128), lambda i: (0, 0), **res),
            pl.BlockSpec((S, NKV_A * DH_A), lambda i: (0, 0), **res),
            pl.BlockSpec((NKV_A * VT_ROWS, S), lambda i: (0, 0), **res),
            pl.BlockSpec((1, DH_A), lambda i: (0, 0)),
            pl.BlockSpec((NH_A, 2 * KBLK, QBLK), lambda i: (0, 0, 0), **res),
            pl.BlockSpec((NH_A, QBLK), lambda i: (0, 0)),
        ],
        out_specs=pl.BlockSpec((QBLK, WQ), lambda i: (i, 0)),
        out_shape=jax.ShapeDtypeStruct((S, WQ), BF16),
        scratch_shapes=[
            pltpu.VMEM((S, QBLK), I32),
            pltpu.VMEM((NH_A * QBLK, DH_A), BF16),
            pltpu.VMEM((NH_I // 2 * QBLK, 128), BF16),
            pltpu.VMEM((8, QBLK), I32),
            pltpu.VMEM((8, QBLK), I32),
            pltpu.VMEM((NH_A, QBLK), F32),
            pltpu.VMEM((NKV_A, VT_ROWS, GRP_A * QBLK), F32),
        ],
        compiler_params=_cparams(("arbitrary",)),
        name="dsa",
    )(z, z, zs, kie, kio, kan, vta, qn_row, tz, bfar)


def _t5_bucket(dist):
    n = jnp.maximum(dist, 0)
    max_exact = N_BUCKETS // 2
    nf = jnp.maximum(n, max_exact).astype(F32)
    large = max_exact + (jnp.log(nf / max_exact) / math.log(MAX_DIST / max_exact)
                         * (N_BUCKETS - max_exact)).astype(I32)
    large = jnp.minimum(large, N_BUCKETS - 1)
    return jnp.where(n < max_exact, n, large)


def _bias_tables(rel_bias):
    cc = jnp.arange(2 * KBLK, dtype=I32)[:, None]
    qq = jnp.arange(QBLK, dtype=I32)[None, :]
    bucket = _t5_bucket(qq - cc + KBLK)
    onehot = (bucket[None] == jnp.arange(N_BUCKETS, dtype=I32)[:, None, None]).astype(F32)
    rb = rel_bias.astype(F32) * LOG2E
    tz = jnp.einsum('bcq,bh->hcq', onehot, rb, precision=lax.Precision.HIGHEST)
    bfar = jnp.broadcast_to(rb[N_BUCKETS - 1][:, None], (NH_A, QBLK))
    return tz, bfar


def _merge_kernel(ym_ref, ya_ref, wm_ref, wa_ref, gm_ref, ga_ref, o_ref):
    a = _dot(ym_ref[...], wm_ref[0])
    b = _dot(ya_ref[...], wa_ref[0])
    o_ref[...] = (_sigmoid(gm_ref[...]) * a + _sigmoid(ga_ref[...]) * b).astype(o_ref.dtype)


def _merge_call(l, ym, ya, wpm, wpa, z):
    S = ym.shape[0]
    tm, tn = 1024, 512
    D = D_MODEL
    return pl.pallas_call(
        _merge_kernel,
        grid=(S // tm, D // tn),
        in_specs=[
            pl.BlockSpec((tm, D), lambda m, n: (m, 0)),
            pl.BlockSpec((tm, D), lambda m, n: (m, 0)),
            pl.BlockSpec((1, D, tn), lambda m, n: (l, 0, n)),
            pl.BlockSpec((1, D, tn), lambda m, n: (l, 0, n)),
            pl.BlockSpec((tm, tn), lambda m, n: (m, C_GM // tn + n)),
            pl.BlockSpec((tm, tn), lambda m, n: (m, C_GA // tn + n)),
        ],
        out_specs=pl.BlockSpec((tm, tn), lambda m, n: (m, n)),
        out_shape=jax.ShapeDtypeStruct((S, D), BF16),
        compiler_params=_cparams(("arbitrary", "arbitrary")),
        name="merge",
    )(ym, ya, wpm, wpa, z, z)


def _out_kernel(mg_ref, wo_ref, x_ref, g1_ref, nw_ref, sc_ref, sh_ref, wr_ref, br_ref,
                xo_ref, h2_ref, lg_ref):
    xn = x_ref[...] + g1_ref[...] * _dot(mg_ref[...], wo_ref[0])
    xo_ref[...] = xn
    r = lax.rsqrt(jnp.mean(xn * xn, axis=-1, keepdims=True) + EPS)
    h = (xn * r) * nw_ref[...] * (1.0 + sc_ref[...]) + sh_ref[...]
    h2_ref[...] = h
    lg_ref[...] = _dot3(h, wr_ref[...]) + br_ref[...]


def _out_call(l, merged, wo, x2, g1, nw, sc, sh, w_router, b_router):
    S = x2.shape[0]
    tm = 256
    D = D_MODEL
    row = lambda m: (0, 0)
    return pl.pallas_call(
        _out_kernel,
        grid=(S // tm,),
        in_specs=[
            pl.BlockSpec((tm, D), lambda m: (m, 0)),
            pl.BlockSpec((1, D, D), lambda m: (l, 0, 0), pipeline_mode=pl.Buffered(1)),
            pl.BlockSpec((tm, D), lambda m: (m, 0)),
            pl.BlockSpec((1, D), row), pl.BlockSpec((1, D), row),
            pl.BlockSpec((1, D), row), pl.BlockSpec((1, D), row),
            pl.BlockSpec((D, 128), row),
            pl.BlockSpec((1, 128), row),
        ],
        out_specs=[
            pl.BlockSpec((tm, D), lambda m: (m, 0)),
            pl.BlockSpec((tm, D), lambda m: (m, 0)),
            pl.BlockSpec((tm, 128), lambda m: (m, 0)),
        ],
        out_shape=[
            jax.ShapeDtypeStruct((S, D), F32),
            jax.ShapeDtypeStruct((S, D), F32),
            jax.ShapeDtypeStruct((S, 128), F32),
        ],
        compiler_params=_cparams(("arbitrary",)),
        name="out_proj",
    )(merged, wo, x2, g1, nw, sc, sh, w_router, b_router)


def _moe_kernel(be_ref, nu_ref, xs_ref, gate_ref, w1_ref, w3_ref, w2_ref, o_ref):
    b = pl.program_id(0)

    @pl.when(b < nu_ref[0])
    def _():
        xb = xs_ref[...].astype(BF16)
        a = _dot(xb, w1_ref[0, 0].astype(BF16))
        c = _dot(xb, w3_ref[0, 0].astype(BF16))
        hmid = (a * _sigmoid(a)) * c
        y = _dot(hmid.astype(BF16), w2_ref[0, 0].astype(BF16))
        o_ref[...] = y * gate_ref[...]

    @pl.when(b >= nu_ref[0])
    def _():
        o_ref[...] = jnp.zeros_like(o_ref)


def _moe_call(l, blk_exp, nused, xs, row_gate, w1, w3, w2):
    rows = xs.shape[0]
    nblk = rows // MOE_BLK
    D = D_MODEL
    grid_spec = pltpu.PrefetchScalarGridSpec(
        num_scalar_prefetch=2,
        grid=(nblk,),
        in_specs=[
            pl.BlockSpec((MOE_BLK, D), lambda b, be, nu: (b, 0)),
            pl.BlockSpec((MOE_BLK, 1), lambda b, be, nu: (b, 0)),
            pl.BlockSpec((1, 1, D, D_FF_E), lambda b, be, nu: (l, be[b], 0, 0)),
            pl.BlockSpec((1, 1, D, D_FF_E), lambda b, be, nu: (l, be[b], 0, 0)),
            pl.BlockSpec((1, 1, D_FF_E, D), lambda b, be, nu: (l, be[b], 0, 0)),
        ],
        out_specs=pl.BlockSpec((MOE_BLK, D), lambda b, be, nu: (b, 0)),
    )
    return pl.pallas_call(
        _moe_kernel,
        grid_spec=grid_spec,
        out_shape=jax.ShapeDtypeStruct((rows, D), F32),
        compiler_params=_cparams(("arbitrary",)),
        name="moe_ffn",
    )(blk_exp, nused, xs, row_gate, w1, w3, w2)


def _route(logits):
    n = logits.shape[0]
    g_logit = logits[:, :N_GROUPS]
    g_sel = jnp.argmax(g_logit, axis=-1)
    p_grp = jnp.take_along_axis(jax.nn.softmax(g_logit, axis=-1), g_sel[:, None], axis=-1)
    e_logit = logits[:, N_GROUPS:N_GROUPS + N_EXPERTS].reshape(n, N_GROUPS, EXP_PER_GROUP)
    e_logit = jnp.take_along_axis(e_logit, g_sel[:, None, None], axis=1)[:, 0]
    top_v, top_i = lax.top_k(e_logit, TOP_K)
    gate = (p_grp * jax.nn.softmax(top_v, axis=-1)).reshape(-1)
    expert = (g_sel[:, None] * EXP_PER_GROUP + top_i).reshape(-1).astype(I32)
    m = n * TOP_K
    onehot = (expert[:, None] == jnp.arange(N_EXPERTS, dtype=I32)[None, :]).astype(I32)
    rank = jnp.take_along_axis(jnp.cumsum(onehot, axis=0) - onehot, expert[:, None], axis=1)[:, 0]
    counts = jnp.sum(onehot, axis=0)
    pcounts = (counts + MOE_BLK - 1) // MOE_BLK * MOE_BLK
    pends = jnp.cumsum(pcounts)
    dest = (pends - pcounts)[expert] + rank
    nblk = -(-m // MOE_BLK) + N_EXPERTS
    rows = nblk * MOE_BLK
    tok = jnp.arange(m, dtype=I32) // TOP_K
    row_tok = jnp.zeros((rows,), I32).at[dest].set(tok)
    row_gate = jnp.zeros((rows,), F32).at[dest].set(gate)
    blk_start = jnp.arange(nblk, dtype=I32)[:, None] * MOE_BLK
    blk_exp = jnp.minimum(jnp.sum((pends[None, :] <= blk_start).astype(I32), axis=1),
                          N_EXPERTS - 1).astype(I32)
    nused = (pends[-1] // MOE_BLK).astype(I32).reshape(1)
    return row_tok, row_gate, blk_exp, nused, dest.reshape(n, TOP_K)


def _layer(l, x2, mod, norm1, norm2, w_big, w_small, conv_w, bg_row, mlstm_norm, q_norm, k_norm,
           tz, bfar, wpm, wpa, wo, w_router, b_router, w1, w3, w2):
    D = D_MODEL
    sh1, sc1, g1, sh2, sc2, g2 = [mod[:, j * D:(j + 1) * D] for j in range(6)]
    z, zs = _in_call(l, x2, norm1[None, :], sc1, sh1, w_big, w_small)
    ym = _mlstm_call(z, zs, conv_w, bg_row, mlstm_norm[None, :])
    kan, vta, kie, kio = _kvprep_call(z, zs, k_norm[None, :])
    ya = _dsa_call(z, zs, kie, kio, kan, vta, q_norm[None, :], tz, bfar)
    merged = _merge_call(l, ym, ya, wpm, wpa, z)
    xn, h2, logits = _out_call(l, merged, wo, x2, g1, norm2[None, :], sc2, sh2, w_router, b_router)
    row_tok, row_gate, blk_exp, nused, dest = _route(logits)
    xs = h2[row_tok]
    yb = _moe_call(l, blk_exp, nused, xs, row_gate[:, None], w1, w3, w2)
    y = yb[dest[:, 0]] + yb[dest[:, 1]]
    return xn + g2 * y


def kernel(x, c, w_ada, b_ada, norm1, norm2, w_in, conv_w, b_gate, mlstm_norm, q_norm, k_norm,
           rel_bias, w_proj_m, w_proj_a, w_out, w_grp, b_grp, w_exp, b_exp, w1, w3, w2):
    B, S, D = x.shape
    assert B == 1 and D == D_MODEL
    x2 = x.reshape(S, D)
    c16 = jnp.pad(c, ((0, 16 - B), (0, 0)))
    mods = _ada_call(c16, w_ada, b_ada[:, None, :])
    tz, bfar = _bias_tables(rel_bias)
    w_big, w_small = _wprep_call(w_in)
    wpm = w_proj_m.astype(BF16)
    wpa = w_proj_a.astype(BF16)
    wo = w_out.astype(BF16)
    for l in range(DEPTH):
        bg_row = jnp.zeros((1, 128), F32).at[0, S_IP:S_IP + 2 * NH_M].set(b_gate[l])
        w_router = jnp.concatenate([w_grp[l], w_exp[l],
                                    jnp.zeros((D, 128 - N_GROUPS - N_EXPERTS), F32)], axis=1)
        b_router = jnp.concatenate([b_grp[l], b_exp[l],
                                    jnp.zeros((128 - N_GROUPS - N_EXPERTS,), F32)])[None, :]
        x2 = _layer(l, x2, mods[l, 0:1, :], norm1[l], norm2[l], w_big, w_small, conv_w[l], bg_row,
                    mlstm_norm[l], q_norm[l], k_norm[l], tz, bfar, wpm, wpa, wo,
                    w_router, b_router, w1, w3, w2)
    return x2.reshape(B, S, D)
```

```python
import functools
import math

import numpy as np
import jax
import jax.numpy as jnp
from jax import lax
from jax.experimental import pallas as pl
from jax.experimental.pallas import tpu as pltpu

F32 = jnp.float32
BF16 = jnp.bfloat16
I32 = jnp.int32

D_MODEL = 2048
DEPTH = 4
NH_M = 8
DH_M = 256
CONV_K = 4
CHUNK = 128
DH_A = 128
NH_A = 16
NKV_A = 4
GRP_A = NH_A // NKV_A
NH_I = 16
DI = 64
TOPK_MAX = 256
QBLK = 128
KBLK = 512
N_BUCKETS = 32
MAX_DIST = 128
N_GROUPS = 4
EXP_PER_GROUP = 8
N_EXPERTS = 32
TOP_K = 2
D_FF_E = 512
MOE_BLK = 128
EPS = 1e-6

IN_SIZES = (2 * NH_M * DH_M, NH_M * DH_M, NH_M * DH_M, NH_M, NH_M, NH_A * DH_A, NKV_A * DH_A,
            NKV_A * DH_A, NH_I * DI, DI, NH_I, D_MODEL, D_MODEL)
D_IN = sum(IN_SIZES)
C_Q, C_K, C_V, C_O = 0, 2048, 4096, 6144
C_QA, C_KA, C_VA, C_QI, C_GM, C_GA = 8192, 10240, 10752, 11264, 12288, 14336
N_BIG = 16384
S_KI, S_WI, S_IP, S_FP = 0, 64, 80, 88

INT_MIN = -(2 ** 31)
NEG = -1e30
LOG2E = math.log2(math.e)
QSCALE = DH_A ** -0.5 * LOG2E
VT_ROWS = DH_A + 16
VMEM_LIMIT = 56 * 1024 * 1024


def _cparams(sem):
    return pltpu.CompilerParams(dimension_semantics=sem, vmem_limit_bytes=VMEM_LIMIT)


def _split(a):
    hi = a.astype(BF16)
    lo = (a - hi.astype(F32)).astype(BF16)
    return hi, lo


def _dot(a, b):
    return jnp.dot(a, b, preferred_element_type=F32)


def _dot3(a, b):
    ah, al = _split(a)
    bh, bl = _split(b)
    return _dot(ah, bh) + (_dot(al, bh) + _dot(ah, bl))


def _dot_nt(a, b):
    return lax.dot_general(a, b, (((1,), (1,)), ((), ())), preferred_element_type=F32)


def _dot_tn(a, b):
    return lax.dot_general(a, b, (((0,), (0,)), ((), ())), preferred_element_type=F32)


def _sigmoid(x):
    return 1.0 / (1.0 + jnp.exp(-x))


def _ada_kernel(c_ref, w_ref, b_ref, o_ref):
    c = c_ref[...]
    ca = c * _sigmoid(c)
    o_ref[0] = _dot3(ca, w_ref[0]) + b_ref[0]


def _ada_call(c16, w_ada, b_ada3):
    tn = 1024
    n6 = w_ada.shape[-1]
    return pl.pallas_call(
        _ada_kernel,
        grid=(DEPTH, n6 // tn),
        in_specs=[
            pl.BlockSpec((16, D_MODEL), lambda l, n: (0, 0)),
            pl.BlockSpec((1, D_MODEL, tn), lambda l, n: (l, 0, n)),
            pl.BlockSpec((1, 1, tn), lambda l, n: (l, 0, n)),
        ],
        out_specs=pl.BlockSpec((1, 16, tn), lambda l, n: (l, 0, n)),
        out_shape=jax.ShapeDtypeStruct((DEPTH, 16, n6), F32),
        compiler_params=_cparams(("arbitrary", "arbitrary")),
        name="adaln",
    )(c16, w_ada, b_ada3)


def _wprep_kernel(w_ref, wb_ref, ws_ref):
    x = w_ref[0]
    cuts = [0] + [int(t) for t in np.cumsum(IN_SIZES)]
    seg = lambda a, b: x[cuts[a]:cuts[b], :]
    wb_ref[0] = jnp.concatenate([seg(0, 3), seg(5, 9), seg(11, 13)], axis=0).T.astype(BF16)
    ws_ref[0] = jnp.concatenate([seg(9, 11), seg(3, 5), jnp.zeros((32, x.shape[1]), F32)], axis=0).T


def _wprep_call(w_in_t):
    tm = 128
    return pl.pallas_call(
        _wprep_kernel,
        grid=(DEPTH, D_MODEL // tm),
        in_specs=[pl.BlockSpec((1, D_IN, tm), lambda l, m: (l, 0, m))],
        out_specs=[
            pl.BlockSpec((1, tm, N_BIG), lambda l, m: (l, m, 0)),
            pl.BlockSpec((1, tm, 128), lambda l, m: (l, m, 0)),
        ],
        out_shape=[
            jax.ShapeDtypeStruct((DEPTH, D_MODEL, N_BIG), BF16),
            jax.ShapeDtypeStruct((DEPTH, D_MODEL, 128), F32),
        ],
        compiler_params=_cparams(("arbitrary", "arbitrary")),
        name="w_prep",
    )(w_in_t)


def _in_kernel(x_ref, nw_ref, sc_ref, sh_ref, wb_ref, ws_ref, z_ref, zs_ref, h_scr):
    @pl.when(pl.program_id(1) == 0)
    def _():
        x = x_ref[...]
        r = lax.rsqrt(jnp.mean(x * x, axis=-1, keepdims=True) + EPS)
        h = (x * r) * nw_ref[...] * (1.0 + sc_ref[...]) + sh_ref[...]
        h_scr[...] = h.astype(BF16)
        zs_ref[...] = _dot3(h, ws_ref[0])

    z_ref[...] = _dot(h_scr[...], wb_ref[0])


def _in_call(l, x2, nw, sc, sh, w_big, w_small):
    S = x2.shape[0]
    tm, tn = 1024, 512
    row = lambda m, n: (0, 0)
    return pl.pallas_call(
        _in_kernel,
        grid=(S // tm, N_BIG // tn),
        in_specs=[
            pl.BlockSpec((tm, D_MODEL), lambda m, n: (m, 0)),
            pl.BlockSpec((1, D_MODEL), row),
            pl.BlockSpec((1, D_MODEL), row),
            pl.BlockSpec((1, D_MODEL), row),
            pl.BlockSpec((1, D_MODEL, tn), lambda m, n: (l, 0, n)),
            pl.BlockSpec((1, D_MODEL, 128), lambda m, n: (l, 0, 0)),
        ],
        out_specs=[
            pl.BlockSpec((tm, tn), lambda m, n: (m, n)),
            pl.BlockSpec((tm, 128), lambda m, n: (m, 0)),
        ],
        out_shape=[
            jax.ShapeDtypeStruct((S, N_BIG), F32),
            jax.ShapeDtypeStruct((S, 128), F32),
        ],
        scratch_shapes=[pltpu.VMEM((tm, D_MODEL), BF16)],
        compiler_params=_cparams(("arbitrary", "arbitrary")),
        name="in_proj",
    )(x2, nw, sc, sh, w_big, w_small)


def _mlstm_kernel(q_ref, k_ref, v_ref, o_ref, zs_ref, cw_ref, bg_ref, nw_ref, y_ref,
                  qprev, kprev, qs, ks, c_st, n_st, m_st):
    L = CHUNK

    @pl.when(pl.program_id(0) == 0)
    def _():
        qprev[...] = jnp.zeros_like(qprev)
        kprev[...] = jnp.zeros_like(kprev)
        c_st[...] = jnp.zeros_like(c_st)
        n_st[...] = jnp.zeros_like(n_st)
        m_st[...] = jnp.zeros_like(m_st)

    row = lax.broadcasted_iota(I32, (L, NH_M * DH_M), 0)

    def conv_silu(cur_ref, prev_ref, woff):
        cur = cur_ref[...]
        prev = prev_ref[...]
        out = None
        for j in range(CONV_K):
            s = CONV_K - 1 - j
            if s == 0:
                u = cur
            else:
                u = jnp.where(row < s, pltpu.roll(prev, s, 0), pltpu.roll(cur, s, 0))
            term = u * cw_ref[j:j + 1, woff:woff + NH_M * DH_M]
            out = term if out is None else out + term
        prev_ref[...] = cur
        return out * _sigmoid(out)

    qs[...] = conv_silu(q_ref, qprev, 0)
    ks[...] = conv_silu(k_ref, kprev, NH_M * DH_M) * (DH_M ** -0.5)

    G = zs_ref[...] + bg_ref[...]
    LF = jnp.minimum(G, 0.0) - jnp.log1p(jnp.exp(-jnp.abs(G)))
    ri = lax.broadcasted_iota(I32, (L, L), 0)
    ci = lax.broadcasted_iota(I32, (L, L), 1)
    causal = ci <= ri
    tril = jnp.where(causal, 1.0, 0.0).astype(BF16)
    lf_hi, lf_lo = _split(LF)
    lf_lo2 = (LF - lf_hi.astype(F32) - lf_lo.astype(F32)).astype(BF16)
    B = _dot(tril, lf_hi) + (_dot(tril, lf_lo) + _dot(tril, lf_lo2))
    GT = G.T
    BT = B.T

    for h in range(NH_M):
        cs = slice(h * DH_M, (h + 1) * DH_M)
        b_col = B[:, S_FP + h:S_FP + h + 1]
        ig_col = G[:, S_IP + h:S_IP + h + 1]
        b_row = BT[S_FP + h:S_FP + h + 1, :]
        ig_row = GT[S_IP + h:S_IP + h + 1, :]
        g = B[L - 1:L, S_FP + h:S_FP + h + 1]
        m_prev = m_st[h:h + 1, 0:1]

        dmat = jnp.where(causal, b_col - b_row + ig_row, -jnp.inf)
        m_intra = jnp.max(dmat, axis=-1, keepdims=True)
        a_col = g - b_col + ig_col
        m_loc = jnp.max(a_col, axis=0, keepdims=True)
        w_loc = jnp.exp(a_col - m_loc)

        qh = qs[:, cs]
        kh = ks[:, cs]
        vh = v_ref[:, cs]
        qb = qh.astype(BF16)
        kb = kh.astype(BF16)
        vb = vh.astype(BF16)
        c_prev = c_st[h]
        n_prev = n_st[h:h + 1, :]

        inter = b_col + m_prev
        m_t = jnp.maximum(inter, m_intra)
        s_inter = jnp.exp(inter - m_t)
        w_intra = jnp.exp(dmat - m_t) * _dot_nt(qb, kb)
        num = _dot(w_intra.astype(BF16), vb) + s_inter * _dot(qb, c_prev.astype(BF16))
        den = (jnp.sum(w_intra, axis=-1, keepdims=True)
               + s_inter * jnp.sum(qh * n_prev, axis=-1, keepdims=True))
        hh = num / jnp.maximum(jnp.abs(den), jnp.exp(-m_t))
        hn = hh * lax.rsqrt(jnp.mean(hh * hh, axis=-1, keepdims=True) + EPS)
        y_ref[:, cs] = (hn * nw_ref[:, cs] * _sigmoid(o_ref[:, cs])).astype(y_ref.dtype)

        kw = kh * w_loc
        c_loc = _dot_tn(kw.astype(BF16), vb)
        n_loc = jnp.sum(kw, axis=0, keepdims=True)
        m_new = jnp.maximum(g + m_prev, m_loc)
        s_p = jnp.exp(g + m_prev - m_new)
        s_l = jnp.exp(m_loc - m_new)
        c_st[h] = s_p * c_prev + s_l * c_loc
        n_st[h:h + 1, :] = s_p * n_prev + s_l * n_loc
        m_st[h:h + 1, :] = jnp.broadcast_to(m_new, (1, 128))


def _mlstm_call(z, zs, conv_w, bg_row, nw_row):
    S = z.shape[0]
    W = NH_M * DH_M
    L = CHUNK
    row = lambda c: (0, 0)
    return pl.pallas_call(
        _mlstm_kernel,
        grid=(S // L,),
        in_specs=[
            pl.BlockSpec((L, W), lambda c: (c, C_Q // W)),
            pl.BlockSpec((L, W), lambda c: (c, C_K // W)),
            pl.BlockSpec((L, W), lambda c: (c, C_V // W)),
            pl.BlockSpec((L, W), lambda c: (c, C_O // W)),
            pl.BlockSpec((L, 128), lambda c: (c, 0)),
            pl.BlockSpec((CONV_K, 2 * W), row),
            pl.BlockSpec((1, 128), row),
            pl.BlockSpec((1, W), row),
        ],
        out_specs=pl.BlockSpec((L, W), lambda c: (c, 0)),
        out_shape=jax.ShapeDtypeStruct((S, W), BF16),
        scratch_shapes=[
            pltpu.VMEM((L, W), F32), pltpu.VMEM((L, W), F32),
            pltpu.VMEM((L, W), F32), pltpu.VMEM((L, W), F32),
            pltpu.VMEM((NH_M, DH_M, DH_M), F32),
            pltpu.VMEM((NH_M, DH_M), F32),
            pltpu.VMEM((NH_M, 128), F32),
        ],
        compiler_params=_cparams(("arbitrary",)),
        name="mlstm",
    )(z, z, z, z, zs, conv_w, bg_row, nw_row)


def _kvprep_kernel(ka_ref, va_ref, zs_ref, kn_ref, kan_ref, vta_ref, kie_ref, kio_ref):
    tm = ka_ref.shape[0]
    for n in range(NKV_A):
        cs = slice(n * DH_A, (n + 1) * DH_A)
        k = ka_ref[:, cs]
        r = lax.rsqrt(jnp.mean(k * k, axis=-1, keepdims=True) + EPS)
        kan_ref[:, cs] = ((k * r) * kn_ref[...]).astype(BF16)
        vta_ref[n * VT_ROWS:n * VT_ROWS + DH_A, :] = va_ref[:, cs].T.astype(BF16)
        vta_ref[n * VT_ROWS + DH_A:(n + 1) * VT_ROWS, :] = jnp.ones((VT_ROWS - DH_A, tm), BF16)
    zs = zs_ref[...]
    lane = lax.broadcasted_iota(I32, zs.shape, 1)
    kie_ref[...] = jnp.where(lane < DI, zs, 0.0).astype(BF16)
    kio_ref[...] = jnp.where(lane >= DI, pltpu.roll(zs, DI, 1), 0.0).astype(BF16)


def _kvprep_call(z, zs, kn_row):
    S = z.shape[0]
    tm = 512
    W = NKV_A * DH_A
    return pl.pallas_call(
        _kvprep_kernel,
        grid=(S // tm,),
        in_specs=[
            pl.BlockSpec((tm, W), lambda m: (m, C_KA // W)),
            pl.BlockSpec((tm, W), lambda m: (m, C_VA // W)),
            pl.BlockSpec((tm, 128), lambda m: (m, 0)),
            pl.BlockSpec((1, DH_A), lambda m: (0, 0)),
        ],
        out_specs=[
            pl.BlockSpec((tm, W), lambda m: (m, 0)),
            pl.BlockSpec((NKV_A * VT_ROWS, tm), lambda m: (0, m)),
            pl.BlockSpec((tm, 128), lambda m: (m, 0)),
            pl.BlockSpec((tm, 128), lambda m: (m, 0)),
        ],
        out_shape=[
            jax.ShapeDtypeStruct((S, W), BF16),
            jax.ShapeDtypeStruct((NKV_A * VT_ROWS, S), BF16),
            jax.ShapeDtypeStruct((S, 128), BF16),
            jax.ShapeDtypeStruct((S, 128), BF16),
        ],
        compiler_params=_cparams(("arbitrary",)),
        name="kv_prep",
    )(z, z, zs, kn_row)


def _sortable(x):
    b = pltpu.bitcast(x, I32)
    return jnp.where(b < 0, b ^ jnp.int32(0x7FFFFFFF), b)


def _dsa_kernel(topk, qi_ref, qa_ref, zs_ref, kie_ref, kio_ref, kan_ref, vta_ref, qn_ref,
                tz_ref, bfar_ref, y_ref,
                skey, qst, qist, thr_s, jcut_s, m_s, acc_s, lg_s):
    i = pl.program_id(0)
    S = skey.shape[0]
    R = KBLK // QBLK
    nkb = i // R + 1
    nfar = jnp.maximum((i - 1) // R, 0)
    keyi = lax.broadcasted_iota(I32, (KBLK, QBLK), 0)
    qcol = lax.broadcasted_iota(I32, (KBLK, QBLK), 1) + i * QBLK

    for p in range(NH_I // 2):
        qist[p * QBLK:(p + 1) * QBLK, :] = qi_ref[:, p * 128:(p + 1) * 128].astype(BF16)
    wT = (zs_ref[...] * ((NH_I * DI) ** -0.5)).T
    for h in range(NH_A):
        q = qa_ref[:, h * DH_A:(h + 1) * DH_A]
        r = lax.rsqrt(jnp.mean(q * q, axis=-1, keepdims=True) + EPS)
        qst[h * QBLK:(h + 1) * QBLK, :] = ((q * r) * qn_ref[...] * QSCALE).astype(BF16)

    def score_block(jb, carry):
        k0 = pl.multiple_of(jb * KBLK, KBLK)
        ke = kie_ref[pl.ds(k0, KBLK), :]
        ko = kio_ref[pl.ds(k0, KBLK), :]

        def pair_scores(c):
            q = qist[c * 2 * QBLK:(c + 1) * 2 * QBLK, :]
            return _dot_nt(ke, q), _dot_nt(ko, q)

        sc = None
        nxt = pair_scores(0)
        for c in range(NH_I // 4):
            se, so = nxt
            if c + 1 < NH_I // 4:
                nxt = pair_scores(c + 1)
            for pp in range(2):
                p = 2 * c + pp
                te = jnp.maximum(se[:, pp * QBLK:(pp + 1) * QBLK], 0.0) * wT[S_WI + 2 * p:S_WI + 2 * p + 1, :]
                to = jnp.maximum(so[:, pp * QBLK:(pp + 1) * QBLK], 0.0) * wT[S_WI + 2 * p + 1:S_WI + 2 * p + 2, :]
                sc = te + to if sc is None else sc + te + to
        sc = jnp.where(sc == 0.0, 0.0, sc)
        valid = (keyi + k0) <= qcol
        skey[pl.ds(k0, KBLK), :] = jnp.where(valid, _sortable(sc), jnp.int32(INT_MIN))
        return carry

    lax.fori_loop(0, nkb, score_block, 0)

    def count(pred_fn):
        def body(jb, cnt):
            k0 = pl.multiple_of(jb * KBLK, KBLK)
            hit = jnp.where(pred_fn(skey[pl.ds(k0, KBLK), :], k0), 1, 0)
            return cnt + jnp.sum(hit.reshape(KBLK // 8, 8, QBLK), axis=0)
        cnt = lax.fori_loop(0, nkb, body, jnp.zeros((8, QBLK), I32))
        return jnp.sum(cnt, axis=0, keepdims=True)

    def bit_pass(b, cand):
        trial = cand ^ jnp.left_shift(jnp.int32(1), 31 - b)
        cnt = count(lambda t, _: t >= trial)
        return jnp.where(cnt >= topk, trial, cand)

    thr = lax.fori_loop(0, 32, bit_pass, jnp.full((1, QBLK), INT_MIN, I32))
    c_gt = count(lambda t, _: t > thr)
    c_eq = count(lambda t, _: t == thr)
    need = topk - c_gt
    short = thr == INT_MIN
    thr_s[...] = jnp.broadcast_to(jnp.where(short, jnp.int32(INT_MIN + 1), thr), (8, QBLK))
    jcut_s[...] = jnp.full((8, QBLK), S, I32)
    tie = jnp.logical_and(jnp.logical_not(short), c_eq > need)

    @pl.when(jnp.max(jnp.where(tie, 1, 0)) > 0)
    def _():
        def idx_pass(b, pos):
            t = pos + jnp.left_shift(jnp.int32(1), 13 - b)
            cnt = count(lambda tl, k0: jnp.logical_and(tl == thr, (keyi + k0) < t))
            return jnp.where(cnt < need, t, pos)
        pos = lax.fori_loop(0, 14, idx_pass, jnp.zeros((1, QBLK), I32))
        jcut_s[...] = jnp.broadcast_to(jnp.where(tie, pos, S), (8, QBLK))

    m_s[...] = jnp.full(m_s.shape, NEG, F32)
    acc_s[...] = jnp.zeros_like(acc_s)

    def logits(kstart, n):
        kn = kan_ref[pl.ds(kstart, KBLK), n * DH_A:(n + 1) * DH_A]
        return _dot_nt(kn, qst[n * GRP_A * QBLK:(n + 1) * GRP_A * QBLK, :])

    def attend(jb, last, near):
        k0 = pl.multiple_of(jb * KBLK, KBLK)
        thr_r = thr_s[0:1, :]
        jc_r = jcut_s[0:1, :]
        sk = skey[pl.ds(k0, KBLK), :]
        sel = jnp.logical_or(sk > thr_r,
                             jnp.logical_and(sk == thr_r, (keyi + k0) <= jc_r))
        negm = jnp.where(sel, 0.0, NEG)
        if near:
            off = pl.multiple_of(KBLK - (i * QBLK - k0), QBLK)

        lg_next = lg_s[...]
        for n in range(NKV_A):
            lg = lg_next
            if n + 1 < NKV_A:
                lg_next = logits(k0, n + 1)
            else:
                k1 = pl.multiple_of(jnp.minimum(jb + 1, last) * KBLK, KBLK)
                lg_s[...] = logits(k1, 0)
            ps, alphas = [], []
            for g in range(GRP_A):
                h = n * GRP_A + g
                x = lg[:, g * QBLK:(g + 1) * QBLK] + negm
                m_old = m_s[h:h + 1, :]
                if near:
                    x = x + tz_ref[h, pl.ds(off, KBLK), :]
                    m_new = jnp.maximum(m_old, jnp.max(x, axis=0, keepdims=True))
                    p = jnp.exp2(x - m_new)
                else:
                    bf = bfar_ref[h:h + 1, :]
                    m_new = jnp.maximum(m_old, jnp.max(x, axis=0, keepdims=True) + bf)
                    p = jnp.exp2(x - (m_new - bf))
                m_s[h:h + 1, :] = m_new
                alphas.append(jnp.exp2(m_old - m_new))
                ps.append(p.astype(BF16))
            pt = jnp.concatenate(ps, axis=1)
            pv = _dot(vta_ref[n * VT_ROWS:(n + 1) * VT_ROWS, pl.ds(k0, KBLK)], pt)
            acc_s[n] = jnp.concatenate(alphas, axis=1) * acc_s[n] + pv

    def far_body(jb, carry):
        attend(jb, nkb - 1, False)
        return carry

    def near_body(jb, carry):
        attend(jb, nkb - 1, True)
        return carry

    lg_s[...] = logits(0, 0)
    lax.fori_loop(0, nfar, far_body, 0)
    lax.fori_loop(nfar, nkb, near_body, 0)

    for h in range(NH_A):
        n, g = divmod(h, GRP_A)
        num = acc_s[n, 0:DH_A, g * QBLK:(g + 1) * QBLK]
        den = acc_s[n, DH_A:DH_A + 1, g * QBLK:(g + 1) * QBLK]
        y_ref[:, h * DH_A:(h + 1) * DH_A] = (num / den).T.astype(y_ref.dtype)


def _dsa_call(z, zs, kie, kio, kan, vta, qn_row, tz, bfar):
    S = z.shape[0]
    topk = min(TOPK_MAX, S // 4)
    WQ = NH_A * DH_A
    WI = NH_I * DI
    res = dict(pipeline_mode=pl.Buffered(1))
    return pl.pallas_call(
        functools.partial(_dsa_kernel, topk),
        grid=(S // QBLK,),
        in_specs=[
            pl.BlockSpec((QBLK, WI), lambda i: (i, C_QI // WI)),
            pl.BlockSpec((QBLK, WQ), lambda i: (i, C_QA // WQ)),
            pl.BlockSpec((QBLK, 128), lambda i: (i, 0)),
            pl.BlockSpec((S, 128), lambda i: (0, 0), **res),
            pl.BlockSpec((S, 128), lambda i: (0, 0), **res),
            pl.BlockSpec((S, NKV_A * DH_A), lambda i: (0, 0), **res),
            pl.BlockSpec((NKV_A * VT_ROWS, S), lambda i: (0, 0), **res),
            pl.BlockSpec((1, DH_A), lambda i: (0, 0)),
            pl.BlockSpec((NH_A, 2 * KBLK, QBLK), lambda i: (0, 0, 0), **res),
            pl.BlockSpec((NH_A, QBLK), lambda i: (0, 0)),
        ],
        out_specs=pl.BlockSpec((QBLK, WQ), lambda i: (i, 0)),
        out_shape=jax.ShapeDtypeStruct((S, WQ), BF16),
        scratch_shapes=[
            pltpu.VMEM((S, QBLK), I32),
            pltpu.VMEM((NH_A * QBLK, DH_A), BF16),
            pltpu.VMEM((NH_I // 2 * QBLK, 128), BF16),
            pltpu.VMEM((8, QBLK), I32),
            pltpu.VMEM((8, QBLK), I32),
            pltpu.VMEM((NH_A, QBLK), F32),
            pltpu.VMEM((NKV_A, VT_ROWS, GRP_A * QBLK), F32),
            pltpu.VMEM((KBLK, GRP_A * QBLK), F32),
        ],
        compiler_params=_cparams(("arbitrary",)),
        name="dsa",
    )(z, z, zs, kie, kio, kan, vta, qn_row, tz, bfar)


def _t5_bucket(dist):
    n = jnp.maximum(dist, 0)
    max_exact = N_BUCKETS // 2
    nf = jnp.maximum(n, max_exact).astype(F32)
    large = max_exact + (jnp.log(nf / max_exact) / math.log(MAX_DIST / max_exact)
                         * (N_BUCKETS - max_exact)).astype(I32)
    large = jnp.minimum(large, N_BUCKETS - 1)
    return jnp.where(n < max_exact, n, large)


def _bias_tables(rel_bias):
    cc = jnp.arange(2 * KBLK, dtype=I32)[:, None]
    qq = jnp.arange(QBLK, dtype=I32)[None, :]
    bucket = _t5_bucket(qq - cc + KBLK)
    onehot = (bucket[None] == jnp.arange(N_BUCKETS, dtype=I32)[:, None, None]).astype(F32)
    rb = rel_bias.astype(F32) * LOG2E
    tz = jnp.einsum('bcq,bh->hcq', onehot, rb, precision=lax.Precision.HIGHEST)
    bfar = jnp.broadcast_to(rb[N_BUCKETS - 1][:, None], (NH_A, QBLK))
    return tz, bfar


def _merge_kernel(ym_ref, ya_ref, wm_ref, wa_ref, gm_ref, ga_ref, o_ref):
    a = _dot(ym_ref[...], wm_ref[0])
    b = _dot(ya_ref[...], wa_ref[0])
    o_ref[...] = (_sigmoid(gm_ref[...]) * a + _sigmoid(ga_ref[...]) * b).astype(o_ref.dtype)


def _merge_call(l, ym, ya, wpm, wpa, z):
    S = ym.shape[0]
    tm, tn = 1024, 512
    D = D_MODEL
    return pl.pallas_call(
        _merge_kernel,
        grid=(S // tm, D // tn),
        in_specs=[
            pl.BlockSpec((tm, D), lambda m, n: (m, 0)),
            pl.BlockSpec((tm, D), lambda m, n: (m, 0)),
            pl.BlockSpec((1, D, tn), lambda m, n: (l, 0, n)),
            pl.BlockSpec((1, D, tn), lambda m, n: (l, 0, n)),
            pl.BlockSpec((tm, tn), lambda m, n: (m, C_GM // tn + n)),
            pl.BlockSpec((tm, tn), lambda m, n: (m, C_GA // tn + n)),
        ],
        out_specs=pl.BlockSpec((tm, tn), lambda m, n: (m, n)),
        out_shape=jax.ShapeDtypeStruct((S, D), BF16),
        compiler_params=_cparams(("arbitrary", "arbitrary")),
        name="merge",
    )(ym, ya, wpm, wpa, z, z)


def _out_kernel(mg_ref, wo_ref, x_ref, g1_ref, nw_ref, sc_ref, sh_ref, wr_ref, br_ref,
                xo_ref, h2_ref, lg_ref):
    xn = x_ref[...] + g1_ref[...] * _dot(mg_ref[...], wo_ref[0])
    xo_ref[...] = xn
    r = lax.rsqrt(jnp.mean(xn * xn, axis=-1, keepdims=True) + EPS)
    h = (xn * r) * nw_ref[...] * (1.0 + sc_ref[...]) + sh_ref[...]
    h2_ref[...] = h
    lg_ref[...] = _dot3(h, wr_ref[...]) + br_ref[...]


def _out_call(l, merged, wo, x2, g1, nw, sc, sh, w_router, b_router):
    S = x2.shape[0]
    tm = 256
    D = D_MODEL
    row = lambda m: (0, 0)
    return pl.pallas_call(
        _out_kernel,
        grid=(S // tm,),
        in_specs=[
            pl.BlockSpec((tm, D), lambda m: (m, 0)),
            pl.BlockSpec((1, D, D), lambda m: (l, 0, 0), pipeline_mode=pl.Buffered(1)),
            pl.BlockSpec((tm, D), lambda m: (m, 0)),
            pl.BlockSpec((1, D), row), pl.BlockSpec((1, D), row),
            pl.BlockSpec((1, D), row), pl.BlockSpec((1, D), row),
            pl.BlockSpec((D, 128), row),
            pl.BlockSpec((1, 128), row),
        ],
        out_specs=[
            pl.BlockSpec((tm, D), lambda m: (m, 0)),
            pl.BlockSpec((tm, D), lambda m: (m, 0)),
            pl.BlockSpec((tm, 128), lambda m: (m, 0)),
        ],
        out_shape=[
            jax.ShapeDtypeStruct((S, D), F32),
            jax.ShapeDtypeStruct((S, D), F32),
            jax.ShapeDtypeStruct((S, 128), F32),
        ],
        compiler_params=_cparams(("arbitrary",)),
        name="out_proj",
    )(merged, wo, x2, g1, nw, sc, sh, w_router, b_router)


def _moe_kernel(be_ref, nu_ref, xs_ref, w1_ref, w3_ref, w2_ref, o_ref):
    b = pl.program_id(0)

    @pl.when(b < nu_ref[0])
    def _():
        xb = xs_ref[...].astype(BF16)
        a = _dot(xb, w1_ref[0, 0].astype(BF16))
        c = _dot(xb, w3_ref[0, 0].astype(BF16))
        hmid = (a * _sigmoid(a)) * c
        o_ref[...] = _dot(hmid.astype(BF16), w2_ref[0, 0].astype(BF16))

    @pl.when(b >= nu_ref[0])
    def _():
        o_ref[...] = jnp.zeros_like(o_ref)


def _combine_kernel(x_ref, y0_ref, y1_ref, gate_ref, g2_ref, o_ref):
    g = gate_ref[...]
    y = y0_ref[...] * g[:, 0:1] + y1_ref[...] * g[:, 1:2]
    o_ref[...] = x_ref[...] + g2_ref[...] * y


def _combine_call(xn, y0, y1, gate, g2):
    S, D = xn.shape
    tm = 512
    blk = pl.BlockSpec((tm, D), lambda m: (m, 0))
    return pl.pallas_call(
        _combine_kernel,
        grid=(S // tm,),
        in_specs=[blk, blk, blk,
                  pl.BlockSpec((tm, TOP_K), lambda m: (m, 0)),
                  pl.BlockSpec((1, D), lambda m: (0, 0))],
        out_specs=blk,
        out_shape=jax.ShapeDtypeStruct((S, D), F32),
        compiler_params=_cparams(("arbitrary",)),
        name="moe_combine",
    )(xn, y0, y1, gate, g2)


def _moe_call(l, blk_exp, nused, xs, w1, w3, w2):
    rows = xs.shape[0]
    nblk = rows // MOE_BLK
    D = D_MODEL
    grid_spec = pltpu.PrefetchScalarGridSpec(
        num_scalar_prefetch=2,
        grid=(nblk,),
        in_specs=[
            pl.BlockSpec((MOE_BLK, D), lambda b, be, nu: (b, 0)),
            pl.BlockSpec((1, 1, D, D_FF_E), lambda b, be, nu: (l, be[b], 0, 0)),
            pl.BlockSpec((1, 1, D, D_FF_E), lambda b, be, nu: (l, be[b], 0, 0)),
            pl.BlockSpec((1, 1, D_FF_E, D), lambda b, be, nu: (l, be[b], 0, 0)),
        ],
        out_specs=pl.BlockSpec((MOE_BLK, D), lambda b, be, nu: (b, 0)),
    )
    return pl.pallas_call(
        _moe_kernel,
        grid_spec=grid_spec,
        out_shape=jax.ShapeDtypeStruct((rows, D), F32),
        compiler_params=_cparams(("arbitrary",)),
        name="moe_ffn",
    )(blk_exp, nused, xs, w1, w3, w2)


def _route(logits):
    n = logits.shape[0]
    g_logit = logits[:, :N_GROUPS]
    g_sel = jnp.argmax(g_logit, axis=-1)
    g_oh = g_sel[:, None] == jnp.arange(N_GROUPS)[None, :]
    p_grp = jnp.sum(jnp.where(g_oh, jax.nn.softmax(g_logit, axis=-1), 0.0), axis=-1, keepdims=True)
    e_logit = logits[:, N_GROUPS:N_GROUPS + N_EXPERTS].reshape(n, N_GROUPS, EXP_PER_GROUP)
    e_logit = jnp.sum(jnp.where(g_oh[:, :, None], e_logit, 0.0), axis=1)
    top_v, top_i = lax.top_k(e_logit, TOP_K)
    gate = p_grp * jax.nn.softmax(top_v, axis=-1)
    expert = (g_sel[:, None] * EXP_PER_GROUP + top_i).reshape(-1).astype(I32)
    m = n * TOP_K
    onehot = (expert[:, None] == jnp.arange(N_EXPERTS, dtype=I32)[None, :]).astype(I32)
    rank = jnp.sum((jnp.cumsum(onehot, axis=0) - onehot) * onehot, axis=1)
    counts = jnp.sum(onehot, axis=0)
    pcounts = (counts + MOE_BLK - 1) // MOE_BLK * MOE_BLK
    pends = jnp.cumsum(pcounts)
    dest = (pends - pcounts)[expert] + rank
    nblk = -(-m // MOE_BLK) + N_EXPERTS
    rows = nblk * MOE_BLK
    tok = jnp.arange(m, dtype=I32) // TOP_K
    row_tok = jnp.zeros((rows,), I32).at[dest].set(tok)
    blk_start = jnp.arange(nblk, dtype=I32)[:, None] * MOE_BLK
    blk_exp = jnp.minimum(jnp.sum((pends[None, :] <= blk_start).astype(I32), axis=1),
                          N_EXPERTS - 1).astype(I32)
    nused = (pends[-1] // MOE_BLK).astype(I32).reshape(1)
    return row_tok, gate, blk_exp, nused, dest.reshape(n, TOP_K)


def _layer(l, x2, mod, norm1, norm2, w_big, w_small, conv_w, bg_row, mlstm_norm, q_norm, k_norm,
           tz, bfar, wpm, wpa, wo, w_router, b_router, w1, w3, w2):
    D = D_MODEL
    sh1, sc1, g1, sh2, sc2, g2 = [mod[:, j * D:(j + 1) * D] for j in range(6)]
    z, zs = _in_call(l, x2, norm1[None, :], sc1, sh1, w_big, w_small)
    ym = _mlstm_call(z, zs, conv_w, bg_row, mlstm_norm[None, :])
    kan, vta, kie, kio = _kvprep_call(z, zs, k_norm[None, :])
    ya = _dsa_call(z, zs, kie, kio, kan, vta, q_norm[None, :], tz, bfar)
    merged = _merge_call(l, ym, ya, wpm, wpa, z)
    xn, h2, logits = _out_call(l, merged, wo, x2, g1, norm2[None, :], sc2, sh2, w_router, b_router)
    row_tok, gate, blk_exp, nused, dest = _route(logits)
    xs = h2[row_tok]
    yb = _moe_call(l, blk_exp, nused, xs, w1, w3, w2)
    return _combine_call(xn, yb[dest[:, 0]], yb[dest[:, 1]], gate, g2)


def kernel(x, c, w_ada, b_ada, norm1, norm2, w_in, conv_w, b_gate, mlstm_norm, q_norm, k_norm,
           rel_bias, w_proj_m, w_proj_a, w_out, w_grp, b_grp, w_exp, b_exp, w1, w3, w2):
    B, S, D = x.shape
    assert B == 1 and D == D_MODEL
    x2 = x.reshape(S, D)
    c16 = jnp.pad(c, ((0, 16 - B), (0, 0)))
    mods = _ada_call(c16, w_ada, b_ada[:, None, :])
    tz, bfar = _bias_tables(rel_bias)
    w_big, w_small = _wprep_call(jnp.swapaxes(w_in, 1, 2))
    wpm = w_proj_m.astype(BF16)
    wpa = w_proj_a.astype(BF16)
    wo = w_out.astype(BF16)
    for l in range(DEPTH):
        bg_row = jnp.zeros((1, 128), F32).at[0, S_IP:S_IP + 2 * NH_M].set(b_gate[l])
        w_router = jnp.concatenate([w_grp[l], w_exp[l],
                                    jnp.zeros((D, 128 - N_GROUPS - N_EXPERTS), F32)], axis=1)
        b_router = jnp.concatenate([b_grp[l], b_exp[l],
                                    jnp.zeros((128 - N_GROUPS - N_EXPERTS,), F32)])[None, :]
        x2 = _layer(l, x2, mods[l, 0:1, :], norm1[l], norm2[l], w_big, w_small, conv_w[l], bg_row,
                    mlstm_norm[l], q_norm[l], k_norm[l], tz, bfar, wpm, wpa, wo,
                    w_router, b_router, w1, w3, w2)
    return x2.reshape(B, S, D)
```

```python
import functools
import math

import numpy as np
import jax
import jax.numpy as jnp
from jax import lax
from jax.experimental import pallas as pl
from jax.experimental.pallas import tpu as pltpu

F32 = jnp.float32
BF16 = jnp.bfloat16
I32 = jnp.int32

D_MODEL = 2048
DEPTH = 4
NH_M = 8
DH_M = 256
CONV_K = 4
CHUNK = 128
DH_A = 128
NH_A = 16
NKV_A = 4
GRP_A = NH_A // NKV_A
NH_I = 16
DI = 64
TOPK_MAX = 256
QBLK = 128
KBLK = 512
N_BUCKETS = 32
MAX_DIST = 128
N_GROUPS = 4
EXP_PER_GROUP = 8
N_EXPERTS = 32
TOP_K = 2
D_FF_E = 512
MOE_BLK = 128
EPS = 1e-6

IN_SIZES = (2 * NH_M * DH_M, NH_M * DH_M, NH_M * DH_M, NH_M, NH_M, NH_A * DH_A, NKV_A * DH_A,
            NKV_A * DH_A, NH_I * DI, DI, NH_I, D_MODEL, D_MODEL)
D_IN = sum(IN_SIZES)
C_Q, C_K, C_V, C_O = 0, 2048, 4096, 6144
C_QA, C_KA, C_VA, C_QI, C_GM, C_GA = 8192, 10240, 10752, 11264, 12288, 14336
N_BIG = 16384
S_KI, S_WI, S_IP, S_FP = 0, 64, 80, 88

INT_MIN = -(2 ** 31)
NEG = -1e30
LOG2E = math.log2(math.e)
QSCALE = DH_A ** -0.5 * LOG2E
VT_ROWS = DH_A + 16
VMEM_LIMIT = 56 * 1024 * 1024


def _cparams(sem):
    return pltpu.CompilerParams(dimension_semantics=sem, vmem_limit_bytes=VMEM_LIMIT)


def _split(a):
    hi = a.astype(BF16)
    lo = (a - hi.astype(F32)).astype(BF16)
    return hi, lo


def _dot(a, b):
    return jnp.dot(a, b, preferred_element_type=F32)


def _dot3(a, b):
    ah, al = _split(a)
    bh, bl = _split(b)
    return _dot(ah, bh) + (_dot(al, bh) + _dot(ah, bl))


def _dot_nt(a, b):
    return lax.dot_general(a, b, (((1,), (1,)), ((), ())), preferred_element_type=F32)


def _dot_tn(a, b):
    return lax.dot_general(a, b, (((0,), (0,)), ((), ())), preferred_element_type=F32)


def _sigmoid(x):
    return 1.0 / (1.0 + jnp.exp(-x))


def _ada_kernel(c_ref, w_ref, b_ref, o_ref):
    c = c_ref[...]
    ca = c * _sigmoid(c)
    o_ref[0] = _dot3(ca, w_ref[0]) + b_ref[0]


def _ada_call(c16, w_ada, b_ada3):
    tn = 1024
    n6 = w_ada.shape[-1]
    return pl.pallas_call(
        _ada_kernel,
        grid=(DEPTH, n6 // tn),
        in_specs=[
            pl.BlockSpec((16, D_MODEL), lambda l, n: (0, 0)),
            pl.BlockSpec((1, D_MODEL, tn), lambda l, n: (l, 0, n)),
            pl.BlockSpec((1, 1, tn), lambda l, n: (l, 0, n)),
        ],
        out_specs=pl.BlockSpec((1, 16, tn), lambda l, n: (l, 0, n)),
        out_shape=jax.ShapeDtypeStruct((DEPTH, 16, n6), F32),
        compiler_params=_cparams(("arbitrary", "arbitrary")),
        name="adaln",
    )(c16, w_ada, b_ada3)


def _wprep_kernel(w_ref, wb_ref, ws_ref):
    x = w_ref[0]
    cuts = [0] + [int(t) for t in np.cumsum(IN_SIZES)]
    seg = lambda a, b: x[cuts[a]:cuts[b], :]
    wb_ref[0] = jnp.concatenate([seg(0, 3), seg(5, 9), seg(11, 13)], axis=0).T.astype(BF16)
    ws_ref[0] = jnp.concatenate([seg(9, 11), seg(3, 5), jnp.zeros((32, x.shape[1]), F32)], axis=0).T


def _wprep_call(w_in_t):
    tm = 128
    return pl.pallas_call(
        _wprep_kernel,
        grid=(DEPTH, D_MODEL // tm),
        in_specs=[pl.BlockSpec((1, D_IN, tm), lambda l, m: (l, 0, m))],
        out_specs=[
            pl.BlockSpec((1, tm, N_BIG), lambda l, m: (l, m, 0)),
            pl.BlockSpec((1, tm, 128), lambda l, m: (l, m, 0)),
        ],
        out_shape=[
            jax.ShapeDtypeStruct((DEPTH, D_MODEL, N_BIG), BF16),
            jax.ShapeDtypeStruct((DEPTH, D_MODEL, 128), F32),
        ],
        compiler_params=_cparams(("arbitrary", "arbitrary")),
        name="w_prep",
    )(w_in_t)


def _in_kernel(x_ref, nw_ref, sc_ref, sh_ref, wb_ref, ws_ref, z_ref, zs_ref, h_scr):
    @pl.when(pl.program_id(1) == 0)
    def _():
        x = x_ref[...]
        r = lax.rsqrt(jnp.mean(x * x, axis=-1, keepdims=True) + EPS)
        h = (x * r) * nw_ref[...] * (1.0 + sc_ref[...]) + sh_ref[...]
        h_scr[...] = h.astype(BF16)
        zs_ref[...] = _dot3(h, ws_ref[0])

    z_ref[...] = _dot(h_scr[...], wb_ref[0])


def _in_call(l, x2, nw, sc, sh, w_big, w_small):
    S = x2.shape[0]
    tm, tn = 1024, 512
    row = lambda m, n: (0, 0)
    return pl.pallas_call(
        _in_kernel,
        grid=(S // tm, N_BIG // tn),
        in_specs=[
            pl.BlockSpec((tm, D_MODEL), lambda m, n: (m, 0)),
            pl.BlockSpec((1, D_MODEL), row),
            pl.BlockSpec((1, D_MODEL), row),
            pl.BlockSpec((1, D_MODEL), row),
            pl.BlockSpec((1, D_MODEL, tn), lambda m, n: (l, 0, n)),
            pl.BlockSpec((1, D_MODEL, 128), lambda m, n: (l, 0, 0)),
        ],
        out_specs=[
            pl.BlockSpec((tm, tn), lambda m, n: (m, n)),
            pl.BlockSpec((tm, 128), lambda m, n: (m, 0)),
        ],
        out_shape=[
            jax.ShapeDtypeStruct((S, N_BIG), F32),
            jax.ShapeDtypeStruct((S, 128), F32),
        ],
        scratch_shapes=[pltpu.VMEM((tm, D_MODEL), BF16)],
        compiler_params=_cparams(("arbitrary", "arbitrary")),
        name="in_proj",
    )(x2, nw, sc, sh, w_big, w_small)


def _mlstm_kernel(q_ref, k_ref, v_ref, o_ref, zs_ref, cw_ref, bg_ref, nw_ref, y_ref,
                  qprev, kprev, qs, ks, c_st, n_st, m_st):
    L = CHUNK

    @pl.when(pl.program_id(0) == 0)
    def _():
        qprev[...] = jnp.zeros_like(qprev)
        kprev[...] = jnp.zeros_like(kprev)
        c_st[...] = jnp.zeros_like(c_st)
        n_st[...] = jnp.zeros_like(n_st)
        m_st[...] = jnp.zeros_like(m_st)

    row8 = lax.broadcasted_iota(I32, (8, NH_M * DH_M), 0)

    def conv_silu(cur_ref, tail_ref, woff):
        cur = cur_ref[...]
        tail = tail_ref[...]
        out = None
        for j in range(CONV_K):
            s = CONV_K - 1 - j
            if s == 0:
                u = cur
            else:
                r = pltpu.roll(cur, s, 0)
                head = jnp.where(row8 < s, pltpu.roll(tail, s, 0), r[0:8])
                u = jnp.concatenate([head, r[8:]], axis=0)
            term = u * cw_ref[j:j + 1, woff:woff + NH_M * DH_M]
            out = term if out is None else out + term
        tail_ref[...] = cur[L - 8:L]
        return out * _sigmoid(out)

    qs[...] = conv_silu(q_ref, qprev, 0)
    ks[...] = conv_silu(k_ref, kprev, NH_M * DH_M) * (DH_M ** -0.5)

    G = zs_ref[...] + bg_ref[...]
    LF = jnp.minimum(G, 0.0) - jnp.log1p(jnp.exp(-jnp.abs(G)))
    ri = lax.broadcasted_iota(I32, (L, L), 0)
    ci = lax.broadcasted_iota(I32, (L, L), 1)
    causal = ci <= ri
    tril = jnp.where(causal, 1.0, 0.0).astype(BF16)
    lf_hi, lf_lo = _split(LF)
    lf_lo2 = (LF - lf_hi.astype(F32) - lf_lo.astype(F32)).astype(BF16)
    B = _dot(tril, lf_hi) + (_dot(tril, lf_lo) + _dot(tril, lf_lo2))
    GT = G.T
    BT = B.T

    for h in range(NH_M):
        cs = slice(h * DH_M, (h + 1) * DH_M)
        b_col = B[:, S_FP + h:S_FP + h + 1]
        ig_col = G[:, S_IP + h:S_IP + h + 1]
        b_row = BT[S_FP + h:S_FP + h + 1, :]
        ig_row = GT[S_IP + h:S_IP + h + 1, :]
        g = B[L - 1:L, S_FP + h:S_FP + h + 1]
        m_prev = m_st[h:h + 1, 0:1]

        dmat = jnp.where(causal, b_col - b_row + ig_row, -jnp.inf)
        m_intra = jnp.max(dmat, axis=-1, keepdims=True)
        a_col = g - b_col + ig_col
        m_loc = jnp.max(a_col, axis=0, keepdims=True)
        w_loc = jnp.exp(a_col - m_loc)

        qh = qs[:, cs]
        kh = ks[:, cs]
        vh = v_ref[:, cs]
        qb = qh.astype(BF16)
        kb = kh.astype(BF16)
        vb = vh.astype(BF16)
        c_prev = c_st[h]
        n_prev = n_st[h:h + 1, :]

        inter = b_col + m_prev
        m_t = jnp.maximum(inter, m_intra)
        s_inter = jnp.exp(inter - m_t)
        w_intra = jnp.exp(dmat - m_t) * _dot_nt(qb, kb)
        num = _dot(w_intra.astype(BF16), vb) + s_inter * _dot(qb, c_prev.astype(BF16))
        den = (jnp.sum(w_intra, axis=-1, keepdims=True)
               + s_inter * jnp.sum(qh * n_prev, axis=-1, keepdims=True))
        hh = num / jnp.maximum(jnp.abs(den), jnp.exp(-m_t))
        hn = hh * lax.rsqrt(jnp.mean(hh * hh, axis=-1, keepdims=True) + EPS)
        y_ref[:, cs] = (hn * nw_ref[:, cs] * _sigmoid(o_ref[:, cs])).astype(y_ref.dtype)

        kw = kh * w_loc
        c_loc = _dot_tn(kw.astype(BF16), vb)
        n_loc = jnp.sum(kw, axis=0, keepdims=True)
        m_new = jnp.maximum(g + m_prev, m_loc)
        s_p = jnp.exp(g + m_prev - m_new)
        s_l = jnp.exp(m_loc - m_new)
        c_st[h] = s_p * c_prev + s_l * c_loc
        n_st[h:h + 1, :] = s_p * n_prev + s_l * n_loc
        m_st[h:h + 1, :] = jnp.broadcast_to(m_new, (1, 128))


def _mlstm_call(z, zs, conv_w, bg_row, nw_row):
    S = z.shape[0]
    W = NH_M * DH_M
    L = CHUNK
    row = lambda c: (0, 0)
    return pl.pallas_call(
        _mlstm_kernel,
        grid=(S // L,),
        in_specs=[
            pl.BlockSpec((L, W), lambda c: (c, C_Q // W)),
            pl.BlockSpec((L, W), lambda c: (c, C_K // W)),
            pl.BlockSpec((L, W), lambda c: (c, C_V // W)),
            pl.BlockSpec((L, W), lambda c: (c, C_O // W)),
            pl.BlockSpec((L, 128), lambda c: (c, 0)),
            pl.BlockSpec((CONV_K, 2 * W), row),
            pl.BlockSpec((1, 128), row),
            pl.BlockSpec((1, W), row),
        ],
        out_specs=pl.BlockSpec((L, W), lambda c: (c, 0)),
        out_shape=jax.ShapeDtypeStruct((S, W), BF16),
        scratch_shapes=[
            pltpu.VMEM((8, W), F32), pltpu.VMEM((8, W), F32),
            pltpu.VMEM((L, W), F32), pltpu.VMEM((L, W), F32),
            pltpu.VMEM((NH_M, DH_M, DH_M), F32),
            pltpu.VMEM((NH_M, DH_M), F32),
            pltpu.VMEM((NH_M, 128), F32),
        ],
        compiler_params=_cparams(("arbitrary",)),
        name="mlstm",
    )(z, z, z, z, zs, conv_w, bg_row, nw_row)


def _kvprep_kernel(ka_ref, va_ref, zs_ref, kn_ref, kan_ref, vta_ref, kie_ref, kio_ref):
    tm = ka_ref.shape[0]
    for n in range(NKV_A):
        cs = slice(n * DH_A, (n + 1) * DH_A)
        k = ka_ref[:, cs]
        r = lax.rsqrt(jnp.mean(k * k, axis=-1, keepdims=True) + EPS)
        kan_ref[:, cs] = ((k * r) * kn_ref[...]).astype(BF16)
        vta_ref[n * VT_ROWS:n * VT_ROWS + DH_A, :] = va_ref[:, cs].T.astype(BF16)
        vta_ref[n * VT_ROWS + DH_A:(n + 1) * VT_ROWS, :] = jnp.ones((VT_ROWS - DH_A, tm), BF16)
    zs = zs_ref[...]
    lane = lax.broadcasted_iota(I32, zs.shape, 1)
    kie_ref[...] = jnp.where(lane < DI, zs, 0.0).astype(BF16)
    kio_ref[...] = jnp.where(lane >= DI, pltpu.roll(zs, DI, 1), 0.0).astype(BF16)


def _kvprep_call(z, zs, kn_row):
    S = z.shape[0]
    tm = 512
    W = NKV_A * DH_A
    return pl.pallas_call(
        _kvprep_kernel,
        grid=(S // tm,),
        in_specs=[
            pl.BlockSpec((tm, W), lambda m: (m, C_KA // W)),
            pl.BlockSpec((tm, W), lambda m: (m, C_VA // W)),
            pl.BlockSpec((tm, 128), lambda m: (m, 0)),
            pl.BlockSpec((1, DH_A), lambda m: (0, 0)),
        ],
        out_specs=[
            pl.BlockSpec((tm, W), lambda m: (m, 0)),
            pl.BlockSpec((NKV_A * VT_ROWS, tm), lambda m: (0, m)),
            pl.BlockSpec((tm, 128), lambda m: (m, 0)),
            pl.BlockSpec((tm, 128), lambda m: (m, 0)),
        ],
        out_shape=[
            jax.ShapeDtypeStruct((S, W), BF16),
            jax.ShapeDtypeStruct((NKV_A * VT_ROWS, S), BF16),
            jax.ShapeDtypeStruct((S, 128), BF16),
            jax.ShapeDtypeStruct((S, 128), BF16),
        ],
        compiler_params=_cparams(("arbitrary",)),
        name="kv_prep",
    )(z, z, zs, kn_row)


def _sortable(x):
    b = pltpu.bitcast(x, I32)
    return jnp.where(b < 0, b ^ jnp.int32(0x7FFFFFFF), b)


def _dsa_kernel(topk, qi_ref, qa_ref, zs_ref, kie_ref, kio_ref, kan_ref, vta_ref, qn_ref,
                tz_ref, bfar_ref, y_ref,
                skey, qst, qist, thr_s, jcut_s, m_s, acc_s, lg_s):
    i = pl.program_id(0)
    S = skey.shape[0]
    R = KBLK // QBLK
    nkb = i // R + 1
    nfar = jnp.maximum((i - 1) // R, 0)
    keyi = lax.broadcasted_iota(I32, (KBLK, QBLK), 0)
    qcol = lax.broadcasted_iota(I32, (KBLK, QBLK), 1) + i * QBLK

    for p in range(NH_I // 2):
        qist[p * QBLK:(p + 1) * QBLK, :] = qi_ref[:, p * 128:(p + 1) * 128].astype(BF16)
    wT = (zs_ref[...] * ((NH_I * DI) ** -0.5)).T
    for h in range(NH_A):
        q = qa_ref[:, h * DH_A:(h + 1) * DH_A]
        r = lax.rsqrt(jnp.mean(q * q, axis=-1, keepdims=True) + EPS)
        qst[h * QBLK:(h + 1) * QBLK, :] = ((q * r) * qn_ref[...] * QSCALE).astype(BF16)

    def score_block(jb, carry):
        k0 = pl.multiple_of(jb * KBLK, KBLK)
        ke = kie_ref[pl.ds(k0, KBLK), :]
        ko = kio_ref[pl.ds(k0, KBLK), :]

        def pair_scores(c):
            q = qist[c * 2 * QBLK:(c + 1) * 2 * QBLK, :]
            return _dot_nt(ke, q), _dot_nt(ko, q)

        sc = None
        nxt = pair_scores(0)
        for c in range(NH_I // 4):
            se, so = nxt
            if c + 1 < NH_I // 4:
                nxt = pair_scores(c + 1)
            for pp in range(2):
                p = 2 * c + pp
                te = jnp.maximum(se[:, pp * QBLK:(pp + 1) * QBLK], 0.0) * wT[S_WI + 2 * p:S_WI + 2 * p + 1, :]
                to = jnp.maximum(so[:, pp * QBLK:(pp + 1) * QBLK], 0.0) * wT[S_WI + 2 * p + 1:S_WI + 2 * p + 2, :]
                sc = te + to if sc is None else sc + te + to
        sc = jnp.where(sc == 0.0, 0.0, sc)
        valid = (keyi + k0) <= qcol
        skey[pl.ds(k0, KBLK), :] = jnp.where(valid, _sortable(sc), jnp.int32(INT_MIN))
        return carry

    lax.fori_loop(0, nkb, score_block, 0)

    def count(pred_fn):
        def body(jb, cnt):
            k0 = pl.multiple_of(jb * KBLK, KBLK)
            hit = jnp.where(pred_fn(skey[pl.ds(k0, KBLK), :], k0), 1, 0)
            return cnt + jnp.sum(hit.reshape(KBLK // 8, 8, QBLK), axis=0)
        cnt = lax.fori_loop(0, nkb, body, jnp.zeros((8, QBLK), I32))
        return jnp.sum(cnt, axis=0, keepdims=True)

    def bit_pass(b, cand):
        trial = cand ^ jnp.left_shift(jnp.int32(1), 31 - b)
        cnt = count(lambda t, _: t >= trial)
        return jnp.where(cnt >= topk, trial, cand)

    thr = lax.fori_loop(0, 32, bit_pass, jnp.full((1, QBLK), INT_MIN, I32))
    c_gt = count(lambda t, _: t > thr)
    c_eq = count(lambda t, _: t == thr)
    need = topk - c_gt
    short = thr == INT_MIN
    thr_s[...] = jnp.broadcast_to(jnp.where(short, jnp.int32(INT_MIN + 1), thr), (8, QBLK))
    jcut_s[...] = jnp.full((8, QBLK), S, I32)
    tie = jnp.logical_and(jnp.logical_not(short), c_eq > need)

    @pl.when(jnp.max(jnp.where(tie, 1, 0)) > 0)
    def _():
        def idx_pass(b, pos):
            t = pos + jnp.left_shift(jnp.int32(1), 13 - b)
            cnt = count(lambda tl, k0: jnp.logical_and(tl == thr, (keyi + k0) < t))
            return jnp.where(cnt < need, t, pos)
        pos = lax.fori_loop(0, 14, idx_pass, jnp.zeros((1, QBLK), I32))
        jcut_s[...] = jnp.broadcast_to(jnp.where(tie, pos, S), (8, QBLK))

    m_s[...] = jnp.full(m_s.shape, NEG, F32)
    acc_s[...] = jnp.zeros_like(acc_s)

    def logits(kstart, n):
        kn = kan_ref[pl.ds(kstart, KBLK), n * DH_A:(n + 1) * DH_A]
        return _dot_nt(kn, qst[n * GRP_A * QBLK:(n + 1) * GRP_A * QBLK, :])

    def attend(jb, last, near):
        k0 = pl.multiple_of(jb * KBLK, KBLK)
        thr_r = thr_s[0:1, :]
        jc_r = jcut_s[0:1, :]
        sk = skey[pl.ds(k0, KBLK), :]
        sel = jnp.logical_or(sk > thr_r,
                             jnp.logical_and(sk == thr_r, (keyi + k0) <= jc_r))
        negm = jnp.where(sel, 0.0, NEG)
        if near:
            off = pl.multiple_of(KBLK - (i * QBLK - k0), QBLK)

        lg_next = lg_s[...]
        for n in range(NKV_A):
            lg = lg_next
            if n + 1 < NKV_A:
                lg_next = logits(k0, n + 1)
            else:
                k1 = pl.multiple_of(jnp.minimum(jb + 1, last) * KBLK, KBLK)
                lg_s[...] = logits(k1, 0)
            ps, alphas = [], []
            for g in range(GRP_A):
                h = n * GRP_A + g
                x = lg[:, g * QBLK:(g + 1) * QBLK] + negm
                m_old = m_s[h:h + 1, :]
                if near:
                    x = x + tz_ref[h, pl.ds(off, KBLK), :]
                    m_new = jnp.maximum(m_old, jnp.max(x, axis=0, keepdims=True))
                    p = jnp.exp2(x - m_new)
                else:
                    bf = bfar_ref[h:h + 1, :]
                    m_new = jnp.maximum(m_old, jnp.max(x, axis=0, keepdims=True) + bf)
                    p = jnp.exp2(x - (m_new - bf))
                m_s[h:h + 1, :] = m_new
                alphas.append(jnp.exp2(m_old - m_new))
                ps.append(p.astype(BF16))
            pt = jnp.concatenate(ps, axis=1)
            pv = _dot(vta_ref[n * VT_ROWS:(n + 1) * VT_ROWS, pl.ds(k0, KBLK)], pt)
            acc_s[n] = jnp.concatenate(alphas, axis=1) * acc_s[n] + pv

    def far_body(jb, carry):
        attend(jb, nkb - 1, False)
        return carry

    def near_body(jb, carry):
        attend(jb, nkb - 1, True)
        return carry

    lg_s[...] = logits(0, 0)
    lax.fori_loop(0, nfar, far_body, 0)
    lax.fori_loop(nfar, nkb, near_body, 0)

    for h in range(NH_A):
        n, g = divmod(h, GRP_A)
        num = acc_s[n, 0:DH_A, g * QBLK:(g + 1) * QBLK]
        den = acc_s[n, DH_A:DH_A + 1, g * QBLK:(g + 1) * QBLK]
        y_ref[:, h * DH_A:(h + 1) * DH_A] = (num / den).T.astype(y_ref.dtype)


def _dsa_call(z, zs, kie, kio, kan, vta, qn_row, tz, bfar):
    S = z.shape[0]
    topk = min(TOPK_MAX, S // 4)
    WQ = NH_A * DH_A
    WI = NH_I * DI
    res = dict(pipeline_mode=pl.Buffered(1))
    return pl.pallas_call(
        functools.partial(_dsa_kernel, topk),
        grid=(S // QBLK,),
        in_specs=[
            pl.BlockSpec((QBLK, WI), lambda i: (i, C_QI // WI)),
            pl.BlockSpec((QBLK, WQ), lambda i: (i, C_QA // WQ)),
            pl.BlockSpec((QBLK, 128), lambda i: (i, 0)),
            pl.BlockSpec((S, 128), lambda i: (0, 0), **res),
            pl.BlockSpec((S, 128), lambda i: (0, 0), **res),
            pl.BlockSpec((S, NKV_A * DH_A), lambda i: (0, 0), **res),
            pl.BlockSpec((NKV_A * VT_ROWS, S), lambda i: (0, 0), **res),
            pl.BlockSpec((1, DH_A), lambda i: (0, 0)),
            pl.BlockSpec((NH_A, 2 * KBLK, QBLK), lambda i: (0, 0, 0), **res),
            pl.BlockSpec((NH_A, QBLK), lambda i: (0, 0)),
        ],
        out_specs=pl.BlockSpec((QBLK, WQ), lambda i: (i, 0)),
        out_shape=jax.ShapeDtypeStruct((S, WQ), BF16),
        scratch_shapes=[
            pltpu.VMEM((S, QBLK), I32),
            pltpu.VMEM((NH_A * QBLK, DH_A), BF16),
            pltpu.VMEM((NH_I // 2 * QBLK, 128), BF16),
            pltpu.VMEM((8, QBLK), I32),
            pltpu.VMEM((8, QBLK), I32),
            pltpu.VMEM((NH_A, QBLK), F32),
            pltpu.VMEM((NKV_A, VT_ROWS, GRP_A * QBLK), F32),
            pltpu.VMEM((KBLK, GRP_A * QBLK), F32),
        ],
        compiler_params=_cparams(("arbitrary",)),
        name="dsa",
    )(z, z, zs, kie, kio, kan, vta, qn_row, tz, bfar)


def _t5_bucket(dist):
    n = jnp.maximum(dist, 0)
    max_exact = N_BUCKETS // 2
    nf = jnp.maximum(n, max_exact).astype(F32)
    large = max_exact + (jnp.log(nf / max_exact) / math.log(MAX_DIST / max_exact)
                         * (N_BUCKETS - max_exact)).astype(I32)
    large = jnp.minimum(large, N_BUCKETS - 1)
    return jnp.where(n < max_exact, n, large)


def _bias_tables(rel_bias):
    cc = jnp.arange(2 * KBLK, dtype=I32)[:, None]
    qq = jnp.arange(QBLK, dtype=I32)[None, :]
    bucket = _t5_bucket(qq - cc + KBLK)
    onehot = (bucket[None] == jnp.arange(N_BUCKETS, dtype=I32)[:, None, None]).astype(F32)
    rb = rel_bias.astype(F32) * LOG2E
    tz = jnp.einsum('bcq,bh->hcq', onehot, rb, precision=lax.Precision.HIGHEST)
    bfar = jnp.broadcast_to(rb[N_BUCKETS - 1][:, None], (NH_A, QBLK))
    return tz, bfar


def _merge_kernel(ym_ref, ya_ref, wm_ref, wa_ref, gm_ref, ga_ref, o_ref):
    a = _dot(ym_ref[...], wm_ref[0])
    b = _dot(ya_ref[...], wa_ref[0])
    o_ref[...] = (_sigmoid(gm_ref[...]) * a + _sigmoid(ga_ref[...]) * b).astype(o_ref.dtype)


def _merge_call(l, ym, ya, wpm, wpa, z):
    S = ym.shape[0]
    tm, tn = 1024, 512
    D = D_MODEL
    return pl.pallas_call(
        _merge_kernel,
        grid=(S // tm, D // tn),
        in_specs=[
            pl.BlockSpec((tm, D), lambda m, n: (m, 0)),
            pl.BlockSpec((tm, D), lambda m, n: (m, 0)),
            pl.BlockSpec((1, D, tn), lambda m, n: (l, 0, n)),
            pl.BlockSpec((1, D, tn), lambda m, n: (l, 0, n)),
            pl.BlockSpec((tm, tn), lambda m, n: (m, C_GM // tn + n)),
            pl.BlockSpec((tm, tn), lambda m, n: (m, C_GA // tn + n)),
        ],
        out_specs=pl.BlockSpec((tm, tn), lambda m, n: (m, n)),
        out_shape=jax.ShapeDtypeStruct((S, D), BF16),
        compiler_params=_cparams(("arbitrary", "arbitrary")),
        name="merge",
    )(ym, ya, wpm, wpa, z, z)


def _out_kernel(mg_ref, wo_ref, x_ref, g1_ref, nw_ref, sc_ref, sh_ref, wr_ref, br_ref,
                xo_ref, h2_ref, lg_ref):
    xn = x_ref[...] + g1_ref[...] * _dot(mg_ref[...], wo_ref[0])
    xo_ref[...] = xn
    r = lax.rsqrt(jnp.mean(xn * xn, axis=-1, keepdims=True) + EPS)
    h = (xn * r) * nw_ref[...] * (1.0 + sc_ref[...]) + sh_ref[...]
    h2_ref[...] = _pack_halves(h)
    lg_ref[...] = _dot3(h, wr_ref[...]) + br_ref[...]


def _out_call(l, merged, wo, x2, g1, nw, sc, sh, w_router, b_router):
    S = x2.shape[0]
    tm = 256
    D = D_MODEL
    row = lambda m: (0, 0)
    return pl.pallas_call(
        _out_kernel,
        grid=(S // tm,),
        in_specs=[
            pl.BlockSpec((tm, D), lambda m: (m, 0)),
            pl.BlockSpec((1, D, D), lambda m: (l, 0, 0), pipeline_mode=pl.Buffered(1)),
            pl.BlockSpec((tm, D), lambda m: (m, 0)),
            pl.BlockSpec((1, D), row), pl.BlockSpec((1, D), row),
            pl.BlockSpec((1, D), row), pl.BlockSpec((1, D), row),
            pl.BlockSpec((D, 128), row),
            pl.BlockSpec((1, 128), row),
        ],
        out_specs=[
            pl.BlockSpec((tm, D), lambda m: (m, 0)),
            pl.BlockSpec((tm, D // 2), lambda m: (m, 0)),
            pl.BlockSpec((tm, 128), lambda m: (m, 0)),
        ],
        out_shape=[
            jax.ShapeDtypeStruct((S, D), F32),
            jax.ShapeDtypeStruct((S, D // 2), I32),
            jax.ShapeDtypeStruct((S, 128), F32),
        ],
        compiler_params=_cparams(("arbitrary",)),
        name="out_proj",
    )(merged, wo, x2, g1, nw, sc, sh, w_router, b_router)


def _pack_halves(h):
    n = h.shape[1] // 2
    bits = pltpu.bitcast(h.astype(BF16).astype(F32), I32)
    lo = lax.shift_right_logical(bits[:, :n], jnp.int32(16))
    hi = bits[:, n:] & jnp.int32(-65536)
    return hi | lo


def _unpack_halves(w):
    lo = pltpu.bitcast(lax.shift_left(w, jnp.int32(16)), F32)
    hi = pltpu.bitcast(w & jnp.int32(-65536), F32)
    return jnp.concatenate([lo, hi], axis=1).astype(BF16)


def _moe_kernel(be_ref, nu_ref, xs_ref, w1_ref, w3_ref, w2_ref, o_ref, w1b, w3b, w2b):
    b = pl.program_id(0)
    new_expert = jnp.logical_or(b == 0, be_ref[b] != be_ref[jnp.maximum(b - 1, 0)])

    @pl.when(jnp.logical_and(b < nu_ref[0], new_expert))
    def _():
        w1b[...] = w1_ref[0, 0].astype(BF16)
        w3b[...] = w3_ref[0, 0].astype(BF16)
        w2b[...] = w2_ref[0, 0].astype(BF16)

    @pl.when(b < nu_ref[0])
    def _():
        xb = _unpack_halves(xs_ref[...])
        a = _dot(xb, w1b[...])
        c = _dot(xb, w3b[...])
        hmid = (a * _sigmoid(a)) * c
        o_ref[...] = _dot(hmid.astype(BF16), w2b[...])

    @pl.when(b >= nu_ref[0])
    def _():
        o_ref[...] = jnp.zeros_like(o_ref)


def _combine_kernel(x_ref, y0_ref, y1_ref, gate_ref, g2_ref, o_ref):
    g = gate_ref[...]
    y = y0_ref[...] * g[:, 0:1] + y1_ref[...] * g[:, 1:2]
    o_ref[...] = x_ref[...] + g2_ref[...] * y


def _combine_call(xn, y0, y1, gate, g2):
    S, D = xn.shape
    tm = 512
    blk = pl.BlockSpec((tm, D), lambda m: (m, 0))
    return pl.pallas_call(
        _combine_kernel,
        grid=(S // tm,),
        in_specs=[blk, blk, blk,
                  pl.BlockSpec((tm, TOP_K), lambda m: (m, 0)),
                  pl.BlockSpec((1, D), lambda m: (0, 0))],
        out_specs=blk,
        out_shape=jax.ShapeDtypeStruct((S, D), F32),
        compiler_params=_cparams(("arbitrary",)),
        name="moe_combine",
    )(xn, y0, y1, gate, g2)


def _moe_call(l, blk_exp, nused, xs, w1, w3, w2):
    rows = xs.shape[0]
    nblk = rows // MOE_BLK
    D = D_MODEL
    grid_spec = pltpu.PrefetchScalarGridSpec(
        num_scalar_prefetch=2,
        grid=(nblk,),
        in_specs=[
            pl.BlockSpec((MOE_BLK, D // 2), lambda b, be, nu: (b, 0)),
            pl.BlockSpec((1, 1, D, D_FF_E), lambda b, be, nu: (l, be[b], 0, 0)),
            pl.BlockSpec((1, 1, D, D_FF_E), lambda b, be, nu: (l, be[b], 0, 0)),
            pl.BlockSpec((1, 1, D_FF_E, D), lambda b, be, nu: (l, be[b], 0, 0)),
        ],
        out_specs=pl.BlockSpec((MOE_BLK, D), lambda b, be, nu: (b, 0)),
        scratch_shapes=[pltpu.VMEM((D, D_FF_E), BF16), pltpu.VMEM((D, D_FF_E), BF16),
                        pltpu.VMEM((D_FF_E, D), BF16)],
    )
    return pl.pallas_call(
        _moe_kernel,
        grid_spec=grid_spec,
        out_shape=jax.ShapeDtypeStruct((rows, D), F32),
        compiler_params=_cparams(("arbitrary",)),
        name="moe_ffn",
    )(blk_exp, nused, xs, w1, w3, w2)


def _route(logits):
    n = logits.shape[0]
    g_logit = logits[:, :N_GROUPS]
    g_sel = jnp.argmax(g_logit, axis=-1)
    g_oh = g_sel[:, None] == jnp.arange(N_GROUPS)[None, :]
    p_grp = jnp.sum(jnp.where(g_oh, jax.nn.softmax(g_logit, axis=-1), 0.0), axis=-1, keepdims=True)
    e_logit = logits[:, N_GROUPS:N_GROUPS + N_EXPERTS].reshape(n, N_GROUPS, EXP_PER_GROUP)
    e_logit = jnp.sum(jnp.where(g_oh[:, :, None], e_logit, 0.0), axis=1)
    top_v, top_i = lax.top_k(e_logit, TOP_K)
    gate = p_grp * jax.nn.softmax(top_v, axis=-1)
    expert = (g_sel[:, None] * EXP_PER_GROUP + top_i).reshape(-1).astype(I32)
    m = n * TOP_K
    onehot = (expert[:, None] == jnp.arange(N_EXPERTS, dtype=I32)[None, :]).astype(I32)
    rank = jnp.sum((jnp.cumsum(onehot, axis=0) - onehot) * onehot, axis=1)
    counts = jnp.sum(onehot, axis=0)
    pcounts = (counts + MOE_BLK - 1) // MOE_BLK * MOE_BLK
    pends = jnp.cumsum(pcounts)
    dest = (pends - pcounts)[expert] + rank
    nblk = -(-m // MOE_BLK) + N_EXPERTS
    rows = nblk * MOE_BLK
    tok = jnp.arange(m, dtype=I32) // TOP_K
    row_tok = (jnp.arange(rows, dtype=I32) % n).at[dest].set(tok)
    blk_start = jnp.arange(nblk, dtype=I32)[:, None] * MOE_BLK
    blk_exp = jnp.minimum(jnp.sum((pends[None, :] <= blk_start).astype(I32), axis=1),
                          N_EXPERTS - 1).astype(I32)
    nused = (pends[-1] // MOE_BLK).astype(I32).reshape(1)
    return row_tok, gate, blk_exp, nused, dest.reshape(n, TOP_K)


def _layer(l, x2, mod, norm1, norm2, w_big, w_small, conv_w, bg_row, mlstm_norm, q_norm, k_norm,
           tz, bfar, wpm, wpa, wo, w_router, b_router, w1, w3, w2):
    D = D_MODEL
    sh1, sc1, g1, sh2, sc2, g2 = [mod[:, j * D:(j + 1) * D] for j in range(6)]
    z, zs = _in_call(l, x2, norm1[None, :], sc1, sh1, w_big, w_small)
    ym = _mlstm_call(z, zs, conv_w, bg_row, mlstm_norm[None, :])
    kan, vta, kie, kio = _kvprep_call(z, zs, k_norm[None, :])
    ya = _dsa_call(z, zs, kie, kio, kan, vta, q_norm[None, :], tz, bfar)
    merged = _merge_call(l, ym, ya, wpm, wpa, z)
    xn, h2, logits = _out_call(l, merged, wo, x2, g1, norm2[None, :], sc2, sh2, w_router, b_router)
    row_tok, gate, blk_exp, nused, dest = _route(logits)
    xs = h2[row_tok]
    yb = _moe_call(l, blk_exp, nused, xs, w1, w3, w2)
    return _combine_call(xn, yb[dest[:, 0]], yb[dest[:, 1]], gate, g2)


def kernel(x, c, w_ada, b_ada, norm1, norm2, w_in, conv_w, b_gate, mlstm_norm, q_norm, k_norm,
           rel_bias, w_proj_m, w_proj_a, w_out, w_grp, b_grp, w_exp, b_exp, w1, w3, w2):
    B, S, D = x.shape
    assert B == 1 and D == D_MODEL
    x2 = x.reshape(S, D)
    c16 = jnp.pad(c, ((0, 16 - B), (0, 0)))
    mods = _ada_call(c16, w_ada, b_ada[:, None, :])
    tz, bfar = _bias_tables(rel_bias)
    w_big, w_small = _wprep_call(jnp.swapaxes(w_in, 1, 2))
    wpm = w_proj_m.astype(BF16)
    wpa = w_proj_a.astype(BF16)
    wo = w_out.astype(BF16)
    for l in range(DEPTH):
        bg_row = jnp.zeros((1, 128), F32).at[0, S_IP:S_IP + 2 * NH_M].set(b_gate[l])
        w_router = jnp.concatenate([w_grp[l], w_exp[l],
                                    jnp.zeros((D, 128 - N_GROUPS - N_EXPERTS), F32)], axis=1)
        b_router = jnp.concatenate([b_grp[l], b_exp[l],
                                    jnp.zeros((128 - N_GROUPS - N_EXPERTS,), F32)])[None, :]
        x2 = _layer(l, x2, mods[l, 0:1, :], norm1[l], norm2[l], w_big, w_small, conv_w[l], bg_row,
                    mlstm_norm[l], q_norm[l], k_norm[l], tz, bfar, wpm, wpa, wo,
                    w_router, b_router, w1, w3, w2)
    return x2.reshape(B, S, D)
```

```python
import functools
import math

import numpy as np
import jax
import jax.numpy as jnp
from jax import lax
from jax.experimental import pallas as pl
from jax.experimental.pallas import tpu as pltpu

F32 = jnp.float32
BF16 = jnp.bfloat16
I32 = jnp.int32

D_MODEL = 2048
DEPTH = 4
NH_M = 8
DH_M = 256
CONV_K = 4
CHUNK = 128
DH_A = 128
NH_A = 16
NKV_A = 4
GRP_A = NH_A // NKV_A
NH_I = 16
DI = 64
TOPK_MAX = 256
QBLK = 128
KBLK = 512
N_BUCKETS = 32
MAX_DIST = 128
N_GROUPS = 4
EXP_PER_GROUP = 8
N_EXPERTS = 32
TOP_K = 2
D_FF_E = 512
MOE_BLK = 128
EPS = 1e-6

IN_SIZES = (2 * NH_M * DH_M, NH_M * DH_M, NH_M * DH_M, NH_M, NH_M, NH_A * DH_A, NKV_A * DH_A,
            NKV_A * DH_A, NH_I * DI, DI, NH_I, D_MODEL, D_MODEL)
D_IN = sum(IN_SIZES)
C_Q, C_K, C_V, C_O = 0, 2048, 4096, 6144
C_QA, C_KA, C_VA, C_QI, C_GM, C_GA = 8192, 10240, 10752, 11264, 12288, 14336
N_BIG = 16384
S_KI, S_WI, S_IP, S_FP = 0, 64, 80, 88

INT_MIN = -(2 ** 31)
NEG = -(2.0 ** 100)
LOG2E = math.log2(math.e)
QSCALE = DH_A ** -0.5 * LOG2E
VT_ROWS = DH_A + 16
VMEM_LIMIT = 56 * 1024 * 1024


def _cparams(sem):
    return pltpu.CompilerParams(dimension_semantics=sem, vmem_limit_bytes=VMEM_LIMIT)


def _split(a):
    hi = a.astype(BF16)
    lo = (a - hi.astype(F32)).astype(BF16)
    return hi, lo


def _dot(a, b):
    return jnp.dot(a, b, preferred_element_type=F32)


def _dot3(a, b):
    ah, al = _split(a)
    bh, bl = _split(b)
    return _dot(ah, bh) + (_dot(al, bh) + _dot(ah, bl))


def _dot_nt(a, b):
    return lax.dot_general(a, b, (((1,), (1,)), ((), ())), preferred_element_type=F32)


def _dot_tn(a, b):
    return lax.dot_general(a, b, (((0,), (0,)), ((), ())), preferred_element_type=F32)


def _sigmoid(x):
    return 1.0 / (1.0 + jnp.exp(-x))


def _ada_kernel(c_ref, w_ref, b_ref, o_ref):
    c = c_ref[...]
    ca = c * _sigmoid(c)
    o_ref[0] = _dot3(ca, w_ref[0]) + b_ref[0]


def _ada_call(c16, w_ada, b_ada3):
    tn = 1024
    n6 = w_ada.shape[-1]
    return pl.pallas_call(
        _ada_kernel,
        grid=(DEPTH, n6 // tn),
        in_specs=[
            pl.BlockSpec((16, D_MODEL), lambda l, n: (0, 0)),
            pl.BlockSpec((1, D_MODEL, tn), lambda l, n: (l, 0, n)),
            pl.BlockSpec((1, 1, tn), lambda l, n: (l, 0, n)),
        ],
        out_specs=pl.BlockSpec((1, 16, tn), lambda l, n: (l, 0, n)),
        out_shape=jax.ShapeDtypeStruct((DEPTH, 16, n6), F32),
        compiler_params=_cparams(("arbitrary", "arbitrary")),
        name="adaln",
    )(c16, w_ada, b_ada3)


def _wprep_kernel(w_ref, wb_ref, ws_ref):
    x = w_ref[0]
    cuts = [0] + [int(t) for t in np.cumsum(IN_SIZES)]
    seg = lambda a, b: x[cuts[a]:cuts[b], :]
    wb_ref[0] = jnp.concatenate([seg(0, 3), seg(5, 9), seg(11, 13)], axis=0).T.astype(BF16)
    ws_ref[0] = jnp.concatenate([seg(9, 11), seg(3, 5), jnp.zeros((32, x.shape[1]), F32)], axis=0).T


def _wprep_call(w_in_t):
    tm = 128
    return pl.pallas_call(
        _wprep_kernel,
        grid=(DEPTH, D_MODEL // tm),
        in_specs=[pl.BlockSpec((1, D_IN, tm), lambda l, m: (l, 0, m))],
        out_specs=[
            pl.BlockSpec((1, tm, N_BIG), lambda l, m: (l, m, 0)),
            pl.BlockSpec((1, tm, 128), lambda l, m: (l, m, 0)),
        ],
        out_shape=[
            jax.ShapeDtypeStruct((DEPTH, D_MODEL, N_BIG), BF16),
            jax.ShapeDtypeStruct((DEPTH, D_MODEL, 128), F32),
        ],
        compiler_params=_cparams(("arbitrary", "arbitrary")),
        name="w_prep",
    )(w_in_t)


def _mixer_input(x, nw_ref, sc_ref, sh_ref, ws_ref, h_ref, zs_ref):
    r = lax.rsqrt(jnp.mean(x * x, axis=-1, keepdims=True) + EPS)
    h = (x * r) * nw_ref[...] * (1.0 + sc_ref[...]) + sh_ref[...]
    h_ref[...] = h.astype(BF16)
    zs_ref[...] = _dot3(h, ws_ref[0])


def _prenorm_kernel(x_ref, nw_ref, sc_ref, sh_ref, ws_ref, h_ref, zs_ref):
    _mixer_input(x_ref[...], nw_ref, sc_ref, sh_ref, ws_ref, h_ref, zs_ref)


def _prenorm_call(l, x2, nw, sc, sh, w_small):
    S, D = x2.shape
    tm = 512
    row = lambda m: (0, 0)
    return pl.pallas_call(
        _prenorm_kernel,
        grid=(S // tm,),
        in_specs=[
            pl.BlockSpec((tm, D), lambda m: (m, 0)),
            pl.BlockSpec((1, D), row), pl.BlockSpec((1, D), row), pl.BlockSpec((1, D), row),
            pl.BlockSpec((1, D, 128), lambda m: (l, 0, 0)),
        ],
        out_specs=[pl.BlockSpec((tm, D), lambda m: (m, 0)), pl.BlockSpec((tm, 128), lambda m: (m, 0))],
        out_shape=[jax.ShapeDtypeStruct((S, D), BF16), jax.ShapeDtypeStruct((S, 128), F32)],
        compiler_params=_cparams(("arbitrary",)),
        name="prenorm",
    )(x2, nw, sc, sh, w_small)


def _in_kernel(h_ref, wb_ref, z_ref):
    z_ref[...] = _dot(h_ref[...], wb_ref[0])


def _in_call(l, h, w_big):
    S = h.shape[0]
    tm, tn = 1024, 1024
    return pl.pallas_call(
        _in_kernel,
        grid=(S // tm, N_BIG // tn),
        in_specs=[
            pl.BlockSpec((tm, D_MODEL), lambda m, n: (m, 0)),
            pl.BlockSpec((1, D_MODEL, tn), lambda m, n: (l, 0, n)),
        ],
        out_specs=pl.BlockSpec((tm, tn), lambda m, n: (m, n)),
        out_shape=jax.ShapeDtypeStruct((S, N_BIG), F32),
        compiler_params=_cparams(("arbitrary", "arbitrary")),
        name="in_proj",
    )(h, w_big)


def _mlstm_kernel(q_ref, k_ref, v_ref, o_ref, zs_ref, cw_ref, bg_ref, nw_ref, y_ref,
                  qprev, kprev, qs, ks, c_st, n_st, m_st):
    L = CHUNK

    @pl.when(pl.program_id(0) == 0)
    def _():
        qprev[...] = jnp.zeros_like(qprev)
        kprev[...] = jnp.zeros_like(kprev)
        c_st[...] = jnp.zeros_like(c_st)
        n_st[...] = jnp.zeros_like(n_st)
        m_st[...] = jnp.zeros_like(m_st)

    row8 = lax.broadcasted_iota(I32, (8, NH_M * DH_M), 0)

    def conv_silu(cur_ref, tail_ref, woff):
        cur = cur_ref[...]
        tail = tail_ref[...]
        out = None
        for j in range(CONV_K):
            s = CONV_K - 1 - j
            if s == 0:
                u = cur
            else:
                r = pltpu.roll(cur, s, 0)
                head = jnp.where(row8 < s, pltpu.roll(tail, s, 0), r[0:8])
                u = jnp.concatenate([head, r[8:]], axis=0)
            term = u * cw_ref[j:j + 1, woff:woff + NH_M * DH_M]
            out = term if out is None else out + term
        tail_ref[...] = cur[L - 8:L]
        return out * _sigmoid(out)

    qs[...] = conv_silu(q_ref, qprev, 0)
    ks[...] = conv_silu(k_ref, kprev, NH_M * DH_M) * (DH_M ** -0.5)

    G = zs_ref[...] + bg_ref[...]
    LF = jnp.minimum(G, 0.0) - jnp.log1p(jnp.exp(-jnp.abs(G)))
    ri = lax.broadcasted_iota(I32, (L, L), 0)
    ci = lax.broadcasted_iota(I32, (L, L), 1)
    causal = ci <= ri
    tril = jnp.where(causal, 1.0, 0.0).astype(BF16)
    lf_hi, lf_lo = _split(LF)
    lf_lo2 = (LF - lf_hi.astype(F32) - lf_lo.astype(F32)).astype(BF16)
    B = _dot(tril, lf_hi) + (_dot(tril, lf_lo) + _dot(tril, lf_lo2))
    GT = G.T
    BT = B.T

    for h in range(NH_M):
        cs = slice(h * DH_M, (h + 1) * DH_M)
        b_col = B[:, S_FP + h:S_FP + h + 1]
        ig_col = G[:, S_IP + h:S_IP + h + 1]
        b_row = BT[S_FP + h:S_FP + h + 1, :]
        ig_row = GT[S_IP + h:S_IP + h + 1, :]
        g = B[L - 1:L, S_FP + h:S_FP + h + 1]
        m_prev = m_st[h:h + 1, 0:1]

        dmat = jnp.where(causal, b_col - b_row + ig_row, -jnp.inf)
        m_intra = jnp.max(dmat, axis=-1, keepdims=True)
        a_col = g - b_col + ig_col
        m_loc = jnp.max(a_col, axis=0, keepdims=True)
        w_loc = jnp.exp(a_col - m_loc)

        qh = qs[:, cs]
        kh = ks[:, cs]
        vh = v_ref[:, cs]
        qb = qh.astype(BF16)
        kb = kh.astype(BF16)
        vb = vh.astype(BF16)
        c_prev = c_st[h]
        n_prev = n_st[h:h + 1, :]

        inter = b_col + m_prev
        m_t = jnp.maximum(inter, m_intra)
        s_inter = jnp.exp(inter - m_t)
        w_intra = jnp.exp(dmat - m_t) * _dot_nt(qb, kb)
        num = _dot(w_intra.astype(BF16), vb) + s_inter * _dot(qb, c_prev.astype(BF16))
        den = (jnp.sum(w_intra, axis=-1, keepdims=True)
               + s_inter * jnp.sum(qh * n_prev, axis=-1, keepdims=True))
        hh = num / jnp.maximum(jnp.abs(den), jnp.exp(-m_t))
        hn = hh * lax.rsqrt(jnp.mean(hh * hh, axis=-1, keepdims=True) + EPS)
        y_ref[:, cs] = (hn * nw_ref[:, cs] * _sigmoid(o_ref[:, cs])).astype(y_ref.dtype)

        kw = kh * w_loc
        c_loc = _dot_tn(kw.astype(BF16), vb)
        n_loc = jnp.sum(kw, axis=0, keepdims=True)
        m_new = jnp.maximum(g + m_prev, m_loc)
        s_p = jnp.exp(g + m_prev - m_new)
        s_l = jnp.exp(m_loc - m_new)
        c_st[h] = s_p * c_prev + s_l * c_loc
        n_st[h:h + 1, :] = s_p * n_prev + s_l * n_loc
        m_st[h:h + 1, :] = jnp.broadcast_to(m_new, (1, 128))


def _mlstm_call(z, zs, conv_w, bg_row, nw_row):
    S = z.shape[0]
    W = NH_M * DH_M
    L = CHUNK
    row = lambda c: (0, 0)
    return pl.pallas_call(
        _mlstm_kernel,
        grid=(S // L,),
        in_specs=[
            pl.BlockSpec((L, W), lambda c: (c, C_Q // W)),
            pl.BlockSpec((L, W), lambda c: (c, C_K // W)),
            pl.BlockSpec((L, W), lambda c: (c, C_V // W)),
            pl.BlockSpec((L, W), lambda c: (c, C_O // W)),
            pl.BlockSpec((L, 128), lambda c: (c, 0)),
            pl.BlockSpec((CONV_K, 2 * W), row),
            pl.BlockSpec((1, 128), row),
            pl.BlockSpec((1, W), row),
        ],
        out_specs=pl.BlockSpec((L, W), lambda c: (c, 0)),
        out_shape=jax.ShapeDtypeStruct((S, W), BF16),
        scratch_shapes=[
            pltpu.VMEM((8, W), F32), pltpu.VMEM((8, W), F32),
            pltpu.VMEM((L, W), F32), pltpu.VMEM((L, W), F32),
            pltpu.VMEM((NH_M, DH_M, DH_M), F32),
            pltpu.VMEM((NH_M, DH_M), F32),
            pltpu.VMEM((NH_M, 128), F32),
        ],
        compiler_params=_cparams(("arbitrary",)),
        name="mlstm",
    )(z, z, z, z, zs, conv_w, bg_row, nw_row)


def _kvprep_kernel(ka_ref, va_ref, zs_ref, kn_ref, kan_ref, vta_ref, kie_ref, kio_ref):
    tm = ka_ref.shape[0]
    for n in range(NKV_A):
        cs = slice(n * DH_A, (n + 1) * DH_A)
        k = ka_ref[:, cs]
        r = lax.rsqrt(jnp.mean(k * k, axis=-1, keepdims=True) + EPS)
        kan_ref[:, cs] = ((k * r) * kn_ref[...]).astype(BF16)
        vta_ref[n * VT_ROWS:n * VT_ROWS + DH_A, :] = va_ref[:, cs].T.astype(BF16)
        vta_ref[n * VT_ROWS + DH_A:(n + 1) * VT_ROWS, :] = jnp.ones((VT_ROWS - DH_A, tm), BF16)
    zs = zs_ref[...]
    lane = lax.broadcasted_iota(I32, zs.shape, 1)
    kie_ref[...] = jnp.where(lane < DI, zs, 0.0).astype(BF16)
    kio_ref[...] = jnp.where(lane >= DI, pltpu.roll(zs, DI, 1), 0.0).astype(BF16)


def _kvprep_call(z, zs, kn_row):
    S = z.shape[0]
    tm = 512
    W = NKV_A * DH_A
    return pl.pallas_call(
        _kvprep_kernel,
        grid=(S // tm,),
        in_specs=[
            pl.BlockSpec((tm, W), lambda m: (m, C_KA // W)),
            pl.BlockSpec((tm, W), lambda m: (m, C_VA // W)),
            pl.BlockSpec((tm, 128), lambda m: (m, 0)),
            pl.BlockSpec((1, DH_A), lambda m: (0, 0)),
        ],
        out_specs=[
            pl.BlockSpec((tm, W), lambda m: (m, 0)),
            pl.BlockSpec((NKV_A * VT_ROWS, tm), lambda m: (0, m)),
            pl.BlockSpec((tm, 128), lambda m: (m, 0)),
            pl.BlockSpec((tm, 128), lambda m: (m, 0)),
        ],
        out_shape=[
            jax.ShapeDtypeStruct((S, W), BF16),
            jax.ShapeDtypeStruct((NKV_A * VT_ROWS, S), BF16),
            jax.ShapeDtypeStruct((S, 128), BF16),
            jax.ShapeDtypeStruct((S, 128), BF16),
        ],
        compiler_params=_cparams(("arbitrary",)),
        name="kv_prep",
    )(z, z, zs, kn_row)


def _sortable(x):
    b = pltpu.bitcast(x, I32)
    return jnp.where(b < 0, b ^ jnp.int32(0x7FFFFFFF), b)


def _dsa_kernel(topk, qi_ref, qa_ref, zs_ref, kie_ref, kio_ref, kan_ref, vta_ref, qn_ref,
                tz_ref, bfar_ref, y_ref,
                skey, shi, qst, qist, thr_s, jcut_s, m_s, acc_s, lg_s):
    i = pl.program_id(0)
    S = skey.shape[0]
    R = KBLK // QBLK
    nkb = i // R + 1
    nfar = jnp.maximum((i - 1) // R, 0)
    keyi = lax.broadcasted_iota(I32, (KBLK, QBLK), 0)
    qcol = lax.broadcasted_iota(I32, (KBLK, QBLK), 1) + i * QBLK

    for p in range(NH_I // 2):
        qist[p * QBLK:(p + 1) * QBLK, :] = qi_ref[:, p * 128:(p + 1) * 128].astype(BF16)
    wT = (zs_ref[...] * ((NH_I * DI) ** -0.5)).T
    for h in range(NH_A):
        q = qa_ref[:, h * DH_A:(h + 1) * DH_A]
        r = lax.rsqrt(jnp.mean(q * q, axis=-1, keepdims=True) + EPS)
        qst[h * QBLK:(h + 1) * QBLK, :] = ((q * r) * qn_ref[...] * QSCALE).astype(BF16)

    def score_block(jb, carry):
        k0 = pl.multiple_of(jb * KBLK, KBLK)
        ke = kie_ref[pl.ds(k0, KBLK), :]
        ko = kio_ref[pl.ds(k0, KBLK), :]

        def pair_scores(c):
            q = qist[c * 2 * QBLK:(c + 1) * 2 * QBLK, :]
            return _dot_nt(ke, q), _dot_nt(ko, q)

        sc = None
        nxt = pair_scores(0)
        for c in range(NH_I // 4):
            se, so = nxt
            if c + 1 < NH_I // 4:
                nxt = pair_scores(c + 1)
            for pp in range(2):
                p = 2 * c + pp
                te = jnp.maximum(se[:, pp * QBLK:(pp + 1) * QBLK], 0.0) * wT[S_WI + 2 * p:S_WI + 2 * p + 1, :]
                to = jnp.maximum(so[:, pp * QBLK:(pp + 1) * QBLK], 0.0) * wT[S_WI + 2 * p + 1:S_WI + 2 * p + 2, :]
                sc = te + to if sc is None else sc + te + to
        sc = jnp.where(sc == 0.0, 0.0, sc)
        valid = (keyi + k0) <= qcol
        skey[pl.ds(k0, KBLK), :] = jnp.where(valid, _sortable(sc), jnp.int32(INT_MIN))
        top = pltpu.bitcast(pltpu.bitcast(sc, I32) & jnp.int32(-65536), F32)
        shi[pl.ds(k0, KBLK), :] = jnp.where(valid, top, -jnp.inf).astype(BF16)
        return carry

    lax.fori_loop(0, nkb, score_block, 0)

    PK = 16

    def count_top(trial):
        def body(jb, cnt):
            k0 = pl.multiple_of(jb * KBLK, KBLK)
            t = shi[pl.ds(k0, KBLK), :].reshape(KBLK // PK, PK, QBLK)
            hit = jnp.where(t >= trial[None], jnp.ones((), t.dtype), jnp.zeros((), t.dtype))
            n = KBLK // PK
            while n > 1:
                n //= 2
                hit = hit[:n] + hit[n:]
            return cnt + hit[0].astype(F32)
        cnt = lax.fori_loop(0, nkb, body, jnp.zeros((PK, QBLK), F32))
        return jnp.sum(cnt, axis=0, keepdims=True)

    def top_pass(b, cand):
        trial = cand ^ jnp.left_shift(jnp.int32(1), 31 - b)
        raw = jnp.where(trial < 0, trial ^ jnp.int32(0x7FFFFFFF), trial) & jnp.int32(-65536)
        tf = jnp.broadcast_to(pltpu.bitcast(raw, F32), (PK, QBLK)).astype(shi.dtype)
        cnt = count_top(tf)
        return jnp.where(cnt >= topk, trial, cand)

    cand16 = lax.fori_loop(0, 16, top_pass, jnp.full((1, QBLK), INT_MIN, I32))

    def count(pred_fn):
        def body(jb, cnt):
            k0 = pl.multiple_of(jb * KBLK, KBLK)
            hit = jnp.where(pred_fn(skey[pl.ds(k0, KBLK), :], k0), 1, 0)
            return cnt + jnp.sum(hit.reshape(KBLK // 8, 8, QBLK), axis=0)
        cnt = lax.fori_loop(0, nkb, body, jnp.zeros((8, QBLK), I32))
        return jnp.sum(cnt, axis=0, keepdims=True)

    def bit_pass(b, cand):
        trial = cand ^ jnp.left_shift(jnp.int32(1), 31 - b)
        cnt = count(lambda t, _: t >= trial)
        return jnp.where(cnt >= topk, trial, cand)

    thr = lax.fori_loop(16, 32, bit_pass, cand16)
    c_gt = count(lambda t, _: t > thr)
    c_eq = count(lambda t, _: t == thr)
    need = topk - c_gt
    short = thr == INT_MIN
    thr_s[...] = jnp.broadcast_to(jnp.where(short, jnp.int32(INT_MIN + 1), thr), (8, QBLK))
    jcut_s[...] = jnp.full((8, QBLK), S, I32)
    tie = jnp.logical_and(jnp.logical_not(short), c_eq > need)

    @pl.when(jnp.max(jnp.where(tie, 1, 0)) > 0)
    def _():
        def idx_pass(b, pos):
            t = pos + jnp.left_shift(jnp.int32(1), 13 - b)
            cnt = count(lambda tl, k0: jnp.logical_and(tl == thr, (keyi + k0) < t))
            return jnp.where(cnt < need, t, pos)
        pos = lax.fori_loop(0, 14, idx_pass, jnp.zeros((1, QBLK), I32))
        jcut_s[...] = jnp.broadcast_to(jnp.where(tie, pos, S), (8, QBLK))

    m_s[...] = jnp.full(m_s.shape, NEG, F32)
    acc_s[...] = jnp.zeros_like(acc_s)

    def logits(kstart, n):
        kn = kan_ref[pl.ds(kstart, KBLK), n * DH_A:(n + 1) * DH_A]
        return _dot_nt(kn, qst[n * GRP_A * QBLK:(n + 1) * GRP_A * QBLK, :])

    def attend(jb, last, near):
        k0 = pl.multiple_of(jb * KBLK, KBLK)
        thr_r = thr_s[0:1, :]
        jc_r = jcut_s[0:1, :]
        sk = skey[pl.ds(k0, KBLK), :]
        sel = jnp.logical_or(sk > thr_r,
                             jnp.logical_and(sk == thr_r, (keyi + k0) <= jc_r))
        negm = jnp.where(sel, 0.0, NEG).astype(BF16).reshape(KBLK // PK, PK, QBLK)
        if near:
            off = pl.multiple_of(KBLK - (i * QBLK - k0), QBLK)

        def col_max(x3):
            n = KBLK // PK
            while n > 1:
                n //= 2
                x3 = jnp.maximum(x3[:n], x3[n:])
            return jnp.max(x3[0].astype(F32), axis=0, keepdims=True)

        def packed_row(r):
            return jnp.broadcast_to(r, (PK, QBLK)).astype(BF16)[None]

        lg_next = lg_s[...]
        for n in range(NKV_A):
            lg = lg_next
            if n + 1 < NKV_A:
                lg_next = logits(k0, n + 1)
            else:
                k1 = pl.multiple_of(jnp.minimum(jb + 1, last) * KBLK, KBLK)
                lg_s[...] = logits(k1, 0)
            ps, alphas = [], []
            for g in range(GRP_A):
                h = n * GRP_A + g
                x = lg[:, g * QBLK:(g + 1) * QBLK].astype(BF16).reshape(KBLK // PK, PK, QBLK) + negm
                m_old = m_s[h:h + 1, :]
                if near:
                    x = x + tz_ref[h, pl.ds(off, KBLK), :].reshape(KBLK // PK, PK, QBLK)
                    shift = packed_row(jnp.maximum(m_old, col_max(x)))
                    m_new = shift[0, 0:1, :].astype(F32)
                else:
                    bf = bfar_ref[h:h + 1, :]
                    shift = packed_row(jnp.maximum(m_old, col_max(x) + bf) - bf)
                    m_new = shift[0, 0:1, :].astype(F32) + bf
                p = jnp.exp2(x - shift)
                m_s[h:h + 1, :] = m_new
                alphas.append(jnp.exp2(m_old - m_new))
                ps.append(p.reshape(KBLK, QBLK))
            pt = jnp.concatenate(ps, axis=1)
            pv = _dot(vta_ref[n * VT_ROWS:(n + 1) * VT_ROWS, pl.ds(k0, KBLK)], pt)
            acc_s[n] = jnp.concatenate(alphas, axis=1) * acc_s[n] + pv

    def far_body(jb, carry):
        attend(jb, nkb - 1, False)
        return carry

    def near_body(jb, carry):
        attend(jb, nkb - 1, True)
        return carry

    lg_s[...] = logits(0, 0)
    lax.fori_loop(0, nfar, far_body, 0)
    lax.fori_loop(nfar, nkb, near_body, 0)

    for h in range(NH_A):
        n, g = divmod(h, GRP_A)
        num = acc_s[n, 0:DH_A, g * QBLK:(g + 1) * QBLK]
        den = acc_s[n, DH_A:DH_A + 1, g * QBLK:(g + 1) * QBLK]
        y_ref[:, h * DH_A:(h + 1) * DH_A] = (num / den).T.astype(y_ref.dtype)


def _dsa_call(z, zs, kie, kio, kan, vta, qn_row, tz, bfar):
    S = z.shape[0]
    topk = min(TOPK_MAX, S // 4)
    WQ = NH_A * DH_A
    WI = NH_I * DI
    res = dict(pipeline_mode=pl.Buffered(1))
    return pl.pallas_call(
        functools.partial(_dsa_kernel, topk),
        grid=(S // QBLK,),
        in_specs=[
            pl.BlockSpec((QBLK, WI), lambda i: (i, C_QI // WI)),
            pl.BlockSpec((QBLK, WQ), lambda i: (i, C_QA // WQ)),
            pl.BlockSpec((QBLK, 128), lambda i: (i, 0)),
            pl.BlockSpec((S, 128), lambda i: (0, 0), **res),
            pl.BlockSpec((S, 128), lambda i: (0, 0), **res),
            pl.BlockSpec((S, NKV_A * DH_A), lambda i: (0, 0), **res),
            pl.BlockSpec((NKV_A * VT_ROWS, S), lambda i: (0, 0), **res),
            pl.BlockSpec((1, DH_A), lambda i: (0, 0)),
            pl.BlockSpec((NH_A, 2 * KBLK, QBLK), lambda i: (0, 0, 0), **res),
            pl.BlockSpec((NH_A, QBLK), lambda i: (0, 0)),
        ],
        out_specs=pl.BlockSpec((QBLK, WQ), lambda i: (i, 0)),
        out_shape=jax.ShapeDtypeStruct((S, WQ), BF16),
        scratch_shapes=[
            pltpu.VMEM((S, QBLK), I32),
            pltpu.VMEM((S, QBLK), BF16),
            pltpu.VMEM((NH_A * QBLK, DH_A), BF16),
            pltpu.VMEM((NH_I // 2 * QBLK, 128), BF16),
            pltpu.VMEM((8, QBLK), I32),
            pltpu.VMEM((8, QBLK), I32),
            pltpu.VMEM((NH_A, QBLK), F32),
            pltpu.VMEM((NKV_A, VT_ROWS, GRP_A * QBLK), F32),
            pltpu.VMEM((KBLK, GRP_A * QBLK), F32),
        ],
        compiler_params=_cparams(("arbitrary",)),
        name="dsa",
    )(z, z, zs, kie, kio, kan, vta, qn_row, tz, bfar)


def _t5_bucket(dist):
    n = jnp.maximum(dist, 0)
    max_exact = N_BUCKETS // 2
    nf = jnp.maximum(n, max_exact).astype(F32)
    large = max_exact + (jnp.log(nf / max_exact) / math.log(MAX_DIST / max_exact)
                         * (N_BUCKETS - max_exact)).astype(I32)
    large = jnp.minimum(large, N_BUCKETS - 1)
    return jnp.where(n < max_exact, n, large)


def _bias_tables(rel_bias):
    cc = jnp.arange(2 * KBLK, dtype=I32)[:, None]
    qq = jnp.arange(QBLK, dtype=I32)[None, :]
    bucket = _t5_bucket(qq - cc + KBLK)
    onehot = (bucket[None] == jnp.arange(N_BUCKETS, dtype=I32)[:, None, None]).astype(F32)
    rb = rel_bias.astype(F32) * LOG2E
    tz = jnp.einsum('bcq,bh->hcq', onehot, rb, precision=lax.Precision.HIGHEST)
    bfar = jnp.broadcast_to(rb[N_BUCKETS - 1][:, None], (NH_A, QBLK))
    return tz.astype(BF16), bfar


def _merge_kernel(ym_ref, ya_ref, wm_ref, wa_ref, gm_ref, ga_ref, o_ref):
    a = _dot(ym_ref[...], wm_ref[0])
    b = _dot(ya_ref[...], wa_ref[0])
    o_ref[...] = (_sigmoid(gm_ref[...]) * a + _sigmoid(ga_ref[...]) * b).astype(o_ref.dtype)


def _merge_call(l, ym, ya, wpm, wpa, z):
    S = ym.shape[0]
    tm, tn = 1024, 512
    D = D_MODEL
    return pl.pallas_call(
        _merge_kernel,
        grid=(S // tm, D // tn),
        in_specs=[
            pl.BlockSpec((tm, D), lambda m, n: (m, 0)),
            pl.BlockSpec((tm, D), lambda m, n: (m, 0)),
            pl.BlockSpec((1, D, tn), lambda m, n: (l, 0, n)),
            pl.BlockSpec((1, D, tn), lambda m, n: (l, 0, n)),
            pl.BlockSpec((tm, tn), lambda m, n: (m, C_GM // tn + n)),
            pl.BlockSpec((tm, tn), lambda m, n: (m, C_GA // tn + n)),
        ],
        out_specs=pl.BlockSpec((tm, tn), lambda m, n: (m, n)),
        out_shape=jax.ShapeDtypeStruct((S, D), BF16),
        compiler_params=_cparams(("arbitrary", "arbitrary")),
        name="merge",
    )(ym, ya, wpm, wpa, z, z)


def _out_kernel(mg_ref, wo_ref, x_ref, g1_ref, nw_ref, sc_ref, sh_ref, wr_ref, br_ref,
                xo_ref, h2_ref, lg_ref):
    xn = x_ref[...] + g1_ref[...] * _dot(mg_ref[...], wo_ref[0])
    xo_ref[...] = xn
    r = lax.rsqrt(jnp.mean(xn * xn, axis=-1, keepdims=True) + EPS)
    h = (xn * r) * nw_ref[...] * (1.0 + sc_ref[...]) + sh_ref[...]
    h2_ref[...] = _pack_halves(h)
    lg_ref[...] = _dot3(h, wr_ref[...]) + br_ref[...]


def _out_call(l, merged, wo, x2, g1, nw, sc, sh, w_router, b_router):
    S = x2.shape[0]
    tm = 256
    D = D_MODEL
    row = lambda m: (0, 0)
    return pl.pallas_call(
        _out_kernel,
        grid=(S // tm,),
        in_specs=[
            pl.BlockSpec((tm, D), lambda m: (m, 0)),
            pl.BlockSpec((1, D, D), lambda m: (l, 0, 0), pipeline_mode=pl.Buffered(1)),
            pl.BlockSpec((tm, D), lambda m: (m, 0)),
            pl.BlockSpec((1, D), row), pl.BlockSpec((1, D), row),
            pl.BlockSpec((1, D), row), pl.BlockSpec((1, D), row),
            pl.BlockSpec((D, 128), row),
            pl.BlockSpec((1, 128), row),
        ],
        out_specs=[
            pl.BlockSpec((tm, D), lambda m: (m, 0)),
            pl.BlockSpec((tm, D // 2), lambda m: (m, 0)),
            pl.BlockSpec((tm, 128), lambda m: (m, 0)),
        ],
        out_shape=[
            jax.ShapeDtypeStruct((S, D), F32),
            jax.ShapeDtypeStruct((S, D // 2), I32),
            jax.ShapeDtypeStruct((S, 128), F32),
        ],
        compiler_params=_cparams(("arbitrary",)),
        name="out_proj",
    )(merged, wo, x2, g1, nw, sc, sh, w_router, b_router)


def _pack_halves(h):
    n = h.shape[1] // 2
    bits = pltpu.bitcast(h.astype(BF16).astype(F32), I32)
    lo = lax.shift_right_logical(bits[:, :n], jnp.int32(16))
    hi = bits[:, n:] & jnp.int32(-65536)
    return hi | lo


def _unpack_halves(w):
    lo = pltpu.bitcast(lax.shift_left(w, jnp.int32(16)), F32)
    hi = pltpu.bitcast(w & jnp.int32(-65536), F32)
    return jnp.concatenate([lo, hi], axis=1).astype(BF16)


def _moe_kernel(l, be_ref, nu_ref, first_ref, slot_ref, nxt_ref, xs_ref, w1_hbm, w3_hbm, w2_hbm, o_ref,
                w1s, w3s, w2s, w1b, w3b, w2b, sem):
    b = pl.program_id(0)

    def weight_copies(e, s):
        return (pltpu.make_async_copy(w1_hbm.at[l, e], w1s.at[s], sem.at[s, 0]),
                pltpu.make_async_copy(w3_hbm.at[l, e], w3s.at[s], sem.at[s, 1]),
                pltpu.make_async_copy(w2_hbm.at[l, e], w2s.at[s], sem.at[s, 2]))

    @pl.when(jnp.logical_and(b < nu_ref[0], first_ref[b] == 1))
    def _():
        s = slot_ref[b]
        e = be_ref[b]

        @pl.when(b == 0)
        def _():
            for cp in weight_copies(e, s):
                cp.start()

        for cp in weight_copies(e, s):
            cp.wait()

        @pl.when(nxt_ref[b] >= 0)
        def _():
            for cp in weight_copies(nxt_ref[b], 1 - s):
                cp.start()

        w1b[...] = w1s[s].astype(BF16)
        w3b[...] = w3s[s].astype(BF16)
        w2b[...] = w2s[s].astype(BF16)

    @pl.when(b < nu_ref[0])
    def _():
        xb = _unpack_halves(xs_ref[...])
        a = _dot(xb, w1b[...])
        c = _dot(xb, w3b[...])
        hmid = (a * _sigmoid(a)) * c
        o_ref[...] = _dot(hmid.astype(BF16), w2b[...])

    @pl.when(b >= nu_ref[0])
    def _():
        o_ref[...] = jnp.zeros_like(o_ref)


def _combine_kernel(x_ref, y0_ref, y1_ref, gate_ref, g2_ref, *rest):
    g = gate_ref[...]
    y = y0_ref[...] * g[:, 0:1] + y1_ref[...] * g[:, 1:2]
    xo = x_ref[...] + g2_ref[...] * y
    if len(rest) == 1:
        rest[0][...] = xo
    else:
        nw_ref, sc_ref, sh_ref, ws_ref, o_ref, h_ref, zs_ref = rest
        o_ref[...] = xo
        _mixer_input(xo, nw_ref, sc_ref, sh_ref, ws_ref, h_ref, zs_ref)


def _combine_call(xn, y0, y1, gate, g2, nxt=None):
    S, D = xn.shape
    tm = 512
    blk = pl.BlockSpec((tm, D), lambda m: (m, 0))
    row = pl.BlockSpec((1, D), lambda m: (0, 0))
    in_specs = [blk, blk, blk, pl.BlockSpec((tm, TOP_K), lambda m: (m, 0)), row]
    args = [xn, y0, y1, gate, g2]
    out_specs, out_shape = blk, jax.ShapeDtypeStruct((S, D), F32)
    if nxt is not None:
        ln, nw, sc, sh, w_small = nxt
        in_specs += [row, row, row, pl.BlockSpec((1, D, 128), lambda m: (ln, 0, 0))]
        args += [nw, sc, sh, w_small]
        out_specs = [blk, blk, pl.BlockSpec((tm, 128), lambda m: (m, 0))]
        out_shape = [out_shape, jax.ShapeDtypeStruct((S, D), BF16), jax.ShapeDtypeStruct((S, 128), F32)]
    return pl.pallas_call(
        _combine_kernel,
        grid=(S // tm,),
        in_specs=in_specs,
        out_specs=out_specs,
        out_shape=out_shape,
        compiler_params=_cparams(("arbitrary",)),
        name="moe_combine",
    )(*args)


def _moe_call(l, plan, xs, w1, w3, w2):
    rows = xs.shape[0]
    nblk = rows // MOE_BLK
    D = D_MODEL
    hbm = pl.BlockSpec(memory_space=pl.ANY)
    grid_spec = pltpu.PrefetchScalarGridSpec(
        num_scalar_prefetch=len(plan),
        grid=(nblk,),
        in_specs=[pl.BlockSpec((MOE_BLK, D // 2), lambda b, *_: (b, 0)), hbm, hbm, hbm],
        out_specs=pl.BlockSpec((MOE_BLK, D), lambda b, *_: (b, 0)),
        scratch_shapes=[pltpu.VMEM((2, D, D_FF_E), F32), pltpu.VMEM((2, D, D_FF_E), F32),
                        pltpu.VMEM((2, D_FF_E, D), F32),
                        pltpu.VMEM((D, D_FF_E), BF16), pltpu.VMEM((D, D_FF_E), BF16),
                        pltpu.VMEM((D_FF_E, D), BF16),
                        pltpu.SemaphoreType.DMA((2, 3))],
    )
    return pl.pallas_call(
        functools.partial(_moe_kernel, l),
        grid_spec=grid_spec,
        out_shape=jax.ShapeDtypeStruct((rows, D), F32),
        compiler_params=_cparams(("arbitrary",)),
        name="moe_ffn",
    )(*plan, xs, w1, w3, w2)


def _route(logits):
    n = logits.shape[0]
    g_logit = logits[:, :N_GROUPS]
    g_sel = jnp.argmax(g_logit, axis=-1)
    g_oh = g_sel[:, None] == jnp.arange(N_GROUPS)[None, :]
    p_grp = jnp.sum(jnp.where(g_oh, jax.nn.softmax(g_logit, axis=-1), 0.0), axis=-1, keepdims=True)
    e_logit = logits[:, N_GROUPS:N_GROUPS + N_EXPERTS].reshape(n, N_GROUPS, EXP_PER_GROUP)
    e_logit = jnp.sum(jnp.where(g_oh[:, :, None], e_logit, 0.0), axis=1)
    top_v, top_i = lax.top_k(e_logit, TOP_K)
    gate = p_grp * jax.nn.softmax(top_v, axis=-1)
    expert = (g_sel[:, None] * EXP_PER_GROUP + top_i).reshape(-1).astype(I32)
    m = n * TOP_K
    onehot = (expert[:, None] == jnp.arange(N_EXPERTS, dtype=I32)[None, :]).astype(I32)
    rank = jnp.sum((jnp.cumsum(onehot, axis=0) - onehot) * onehot, axis=1)
    counts = jnp.sum(onehot, axis=0)
    pcounts = (counts + MOE_BLK - 1) // MOE_BLK * MOE_BLK
    pends = jnp.cumsum(pcounts)
    dest = (pends - pcounts)[expert] + rank
    nblk = -(-m // MOE_BLK) + N_EXPERTS
    rows = nblk * MOE_BLK
    tok = jnp.arange(m, dtype=I32) // TOP_K
    row_tok = (jnp.arange(rows, dtype=I32) % n).at[dest].set(tok)
    blk_start = jnp.arange(nblk, dtype=I32)[:, None] * MOE_BLK
    blk_exp = jnp.minimum(jnp.sum((pends[None, :] <= blk_start).astype(I32), axis=1),
                          N_EXPERTS - 1).astype(I32)
    nused = (pends[-1] // MOE_BLK).astype(I32).reshape(1)
    prev = jnp.concatenate([jnp.full((1,), -1, I32), blk_exp[:-1]])
    first = (blk_exp != prev).astype(I32)
    slot = (jnp.cumsum(first) - 1) & 1
    eid = jnp.arange(N_EXPERTS, dtype=I32)
    later = jnp.logical_and(eid[None, :] > eid[:, None], (counts > 0)[None, :])
    nxt_e = jnp.min(jnp.where(later, eid[None, :], N_EXPERTS), axis=1)
    nxt = jnp.where(nxt_e < N_EXPERTS, nxt_e, -1)[blk_exp].astype(I32)
    plan = (blk_exp, nused, first, slot.astype(I32), nxt)
    return row_tok, gate, plan, dest.reshape(n, TOP_K)


def _layer(l, x2, h, zs, mod, nxt, norm2, w_big, conv_w, bg_row, mlstm_norm, q_norm, k_norm,
           tz, bfar, wpm, wpa, wo, w_router, b_router, w1, w3, w2):
    D = D_MODEL
    sh1, sc1, g1, sh2, sc2, g2 = [mod[:, j * D:(j + 1) * D] for j in range(6)]
    z = _in_call(l, h, w_big)
    ym = _mlstm_call(z, zs, conv_w, bg_row, mlstm_norm[None, :])
    kan, vta, kie, kio = _kvprep_call(z, zs, k_norm[None, :])
    ya = _dsa_call(z, zs, kie, kio, kan, vta, q_norm[None, :], tz, bfar)
    merged = _merge_call(l, ym, ya, wpm, wpa, z)
    xn, h2, logits = _out_call(l, merged, wo, x2, g1, norm2[None, :], sc2, sh2, w_router, b_router)
    row_tok, gate, plan, dest = _route(logits)
    xs = h2[row_tok]
    yb = _moe_call(l, plan, xs, w1, w3, w2)
    return _combine_call(xn, yb[dest[:, 0]], yb[dest[:, 1]], gate, g2, nxt)


def kernel(x, c, w_ada, b_ada, norm1, norm2, w_in, conv_w, b_gate, mlstm_norm, q_norm, k_norm,
           rel_bias, w_proj_m, w_proj_a, w_out, w_grp, b_grp, w_exp, b_exp, w1, w3, w2):
    B, S, D = x.shape
    assert B == 1 and D == D_MODEL
    x2 = x.reshape(S, D)
    c16 = jnp.pad(c, ((0, 16 - B), (0, 0)))
    mods = _ada_call(c16, w_ada, b_ada[:, None, :])
    tz, bfar = _bias_tables(rel_bias)
    w_big, w_small = _wprep_call(jnp.swapaxes(w_in, 1, 2))
    wpm = w_proj_m.astype(BF16)
    wpa = w_proj_a.astype(BF16)
    wo = w_out.astype(BF16)

    def norm1_params(l):
        return (l, norm1[l][None, :], mods[l, 0:1, D:2 * D], mods[l, 0:1, 0:D], w_small)

    first = norm1_params(0)
    h, zs = _prenorm_call(first[0], x2, *first[1:])
    for l in range(DEPTH):
        bg_row = jnp.zeros((1, 128), F32).at[0, S_IP:S_IP + 2 * NH_M].set(b_gate[l])
        w_router = jnp.concatenate([w_grp[l], w_exp[l],
                                    jnp.zeros((D, 128 - N_GROUPS - N_EXPERTS), F32)], axis=1)
        b_router = jnp.concatenate([b_grp[l], b_exp[l],
                                    jnp.zeros((128 - N_GROUPS - N_EXPERTS,), F32)])[None, :]
        nxt = norm1_params(l + 1) if l + 1 < DEPTH else None
        out = _layer(l, x2, h, zs, mods[l, 0:1, :], nxt, norm2[l], w_big, conv_w[l], bg_row,
                     mlstm_norm[l], q_norm[l], k_norm[l], tz, bfar, wpm, wpa, wo,
                     w_router, b_router, w1, w3, w2)
        if nxt is None:
            x2 = out
        else:
            x2, h, zs = out
    return x2.reshape(B, S, D)
```

```python
import functools
import math

import numpy as np
import jax
import jax.numpy as jnp
from jax import lax
from jax.experimental import pallas as pl
from jax.experimental.pallas import tpu as pltpu

F32 = jnp.float32
BF16 = jnp.bfloat16
I32 = jnp.int32

D_MODEL = 2048
DEPTH = 4
NH_M = 8
DH_M = 256
CONV_K = 4
CHUNK = 128
DH_A = 128
NH_A = 16
NKV_A = 4
GRP_A = NH_A // NKV_A
NH_I = 16
DI = 64
TOPK_MAX = 256
QBLK = 128
KBLK = 512
N_BUCKETS = 32
MAX_DIST = 128
N_GROUPS = 4
EXP_PER_GROUP = 8
N_EXPERTS = 32
TOP_K = 2
D_FF_E = 512
MOE_BLK = 128
EPS = 1e-6

IN_SIZES = (2 * NH_M * DH_M, NH_M * DH_M, NH_M * DH_M, NH_M, NH_M, NH_A * DH_A, NKV_A * DH_A,
            NKV_A * DH_A, NH_I * DI, DI, NH_I, D_MODEL, D_MODEL)
D_IN = sum(IN_SIZES)
C_Q, C_K, C_V, C_O = 0, 2048, 4096, 6144
C_QA, C_KA, C_VA, C_QI, C_GM, C_GA = 8192, 10240, 10752, 11264, 12288, 14336
N_BIG = 16384
S_KI, S_WI, S_IP, S_FP = 0, 64, 80, 88

INT_MIN = -(2 ** 31)
NEG = -(2.0 ** 100)
LOG2E = math.log2(math.e)
QSCALE = DH_A ** -0.5 * LOG2E
VT_ROWS = DH_A + 16
VMEM_LIMIT = 56 * 1024 * 1024


def _cparams(sem):
    return pltpu.CompilerParams(dimension_semantics=sem, vmem_limit_bytes=VMEM_LIMIT)


def _split(a):
    hi = a.astype(BF16)
    lo = (a - hi.astype(F32)).astype(BF16)
    return hi, lo


def _dot(a, b):
    return jnp.dot(a, b, preferred_element_type=F32)


def _dot3(a, b):
    ah, al = _split(a)
    bh, bl = _split(b)
    return _dot(ah, bh) + (_dot(al, bh) + _dot(ah, bl))


def _dot_nt(a, b):
    return lax.dot_general(a, b, (((1,), (1,)), ((), ())), preferred_element_type=F32)


def _dot_tn(a, b):
    return lax.dot_general(a, b, (((0,), (0,)), ((), ())), preferred_element_type=F32)


def _sigmoid(x):
    return 1.0 / (1.0 + jnp.exp(-x))


def _ada_kernel(c_ref, w_ref, b_ref, o_ref):
    c = c_ref[...]
    ca = c * _sigmoid(c)
    o_ref[0] = _dot3(ca, w_ref[0]) + b_ref[0]


def _ada_call(c16, w_ada, b_ada3):
    tn = 1024
    n6 = w_ada.shape[-1]
    return pl.pallas_call(
        _ada_kernel,
        grid=(DEPTH, n6 // tn),
        in_specs=[
            pl.BlockSpec((16, D_MODEL), lambda l, n: (0, 0)),
            pl.BlockSpec((1, D_MODEL, tn), lambda l, n: (l, 0, n)),
            pl.BlockSpec((1, 1, tn), lambda l, n: (l, 0, n)),
        ],
        out_specs=pl.BlockSpec((1, 16, tn), lambda l, n: (l, 0, n)),
        out_shape=jax.ShapeDtypeStruct((DEPTH, 16, n6), F32),
        compiler_params=_cparams(("arbitrary", "arbitrary")),
        name="adaln",
    )(c16, w_ada, b_ada3)


def _wprep_kernel(w_ref, wb_ref, ws_ref):
    x = w_ref[0]
    cuts = [0] + [int(t) for t in np.cumsum(IN_SIZES)]
    seg = lambda a, b: x[cuts[a]:cuts[b], :]
    wb_ref[0] = jnp.concatenate([seg(0, 3), seg(5, 9), seg(11, 13)], axis=0).T.astype(BF16)
    ws_ref[0] = jnp.concatenate([seg(9, 11), seg(3, 5), jnp.zeros((32, x.shape[1]), F32)], axis=0).T


def _wprep_call(w_in_t):
    tm = 128
    return pl.pallas_call(
        _wprep_kernel,
        grid=(DEPTH, D_MODEL // tm),
        in_specs=[pl.BlockSpec((1, D_IN, tm), lambda l, m: (l, 0, m))],
        out_specs=[
            pl.BlockSpec((1, tm, N_BIG), lambda l, m: (l, m, 0)),
            pl.BlockSpec((1, tm, 128), lambda l, m: (l, m, 0)),
        ],
        out_shape=[
            jax.ShapeDtypeStruct((DEPTH, D_MODEL, N_BIG), BF16),
            jax.ShapeDtypeStruct((DEPTH, D_MODEL, 128), F32),
        ],
        compiler_params=_cparams(("arbitrary", "arbitrary")),
        name="w_prep",
    )(w_in_t)


def _mixer_input(x, nw_ref, sc_ref, sh_ref, ws_ref, h_ref, zs_ref):
    r = lax.rsqrt(jnp.mean(x * x, axis=-1, keepdims=True) + EPS)
    h = (x * r) * nw_ref[...] * (1.0 + sc_ref[...]) + sh_ref[...]
    h_ref[...] = h.astype(BF16)
    zs_ref[...] = _dot3(h, ws_ref[0])


def _prenorm_kernel(x_ref, nw_ref, sc_ref, sh_ref, ws_ref, h_ref, zs_ref):
    _mixer_input(x_ref[...], nw_ref, sc_ref, sh_ref, ws_ref, h_ref, zs_ref)


def _prenorm_call(l, x2, nw, sc, sh, w_small):
    S, D = x2.shape
    tm = 512
    row = lambda m: (0, 0)
    return pl.pallas_call(
        _prenorm_kernel,
        grid=(S // tm,),
        in_specs=[
            pl.BlockSpec((tm, D), lambda m: (m, 0)),
            pl.BlockSpec((1, D), row), pl.BlockSpec((1, D), row), pl.BlockSpec((1, D), row),
            pl.BlockSpec((1, D, 128), lambda m: (l, 0, 0)),
        ],
        out_specs=[pl.BlockSpec((tm, D), lambda m: (m, 0)), pl.BlockSpec((tm, 128), lambda m: (m, 0))],
        out_shape=[jax.ShapeDtypeStruct((S, D), BF16), jax.ShapeDtypeStruct((S, 128), F32)],
        compiler_params=_cparams(("arbitrary",)),
        name="prenorm",
    )(x2, nw, sc, sh, w_small)


def _in_kernel(h_ref, wb_ref, z_ref):
    z_ref[...] = _dot(h_ref[...], wb_ref[0])


def _in_call(l, h, w_big):
    S = h.shape[0]
    tm, tn = 1024, 1024
    return pl.pallas_call(
        _in_kernel,
        grid=(S // tm, N_BIG // tn),
        in_specs=[
            pl.BlockSpec((tm, D_MODEL), lambda m, n: (m, 0)),
            pl.BlockSpec((1, D_MODEL, tn), lambda m, n: (l, 0, n)),
        ],
        out_specs=pl.BlockSpec((tm, tn), lambda m, n: (m, n)),
        out_shape=jax.ShapeDtypeStruct((S, N_BIG), F32),
        compiler_params=_cparams(("arbitrary", "arbitrary")),
        name="in_proj",
    )(h, w_big)


def _mlstm_kernel(q_ref, k_ref, v_ref, o_ref, zs_ref, cw_ref, bg_ref, nw_ref, y_ref,
                  qprev, kprev, qs, ks, c_st, n_st, m_st):
    L = CHUNK

    @pl.when(pl.program_id(0) == 0)
    def _():
        qprev[...] = jnp.zeros_like(qprev)
        kprev[...] = jnp.zeros_like(kprev)
        c_st[...] = jnp.zeros_like(c_st)
        n_st[...] = jnp.zeros_like(n_st)
        m_st[...] = jnp.zeros_like(m_st)

    row8 = lax.broadcasted_iota(I32, (8, NH_M * DH_M), 0)

    def conv_silu(cur_ref, tail_ref, woff):
        cur = cur_ref[...]
        tail = tail_ref[...]
        out = None
        for j in range(CONV_K):
            s = CONV_K - 1 - j
            if s == 0:
                u = cur
            else:
                r = pltpu.roll(cur, s, 0)
                head = jnp.where(row8 < s, pltpu.roll(tail, s, 0), r[0:8])
                u = jnp.concatenate([head, r[8:]], axis=0)
            term = u * cw_ref[j:j + 1, woff:woff + NH_M * DH_M]
            out = term if out is None else out + term
        tail_ref[...] = cur[L - 8:L]
        return out * _sigmoid(out)

    qs[...] = conv_silu(q_ref, qprev, 0)
    ks[...] = conv_silu(k_ref, kprev, NH_M * DH_M) * (DH_M ** -0.5)

    G = zs_ref[...] + bg_ref[...]
    LF = jnp.minimum(G, 0.0) - jnp.log1p(jnp.exp(-jnp.abs(G)))
    ri = lax.broadcasted_iota(I32, (L, L), 0)
    ci = lax.broadcasted_iota(I32, (L, L), 1)
    causal = ci <= ri
    tril = jnp.where(causal, 1.0, 0.0).astype(BF16)
    lf_hi, lf_lo = _split(LF)
    lf_lo2 = (LF - lf_hi.astype(F32) - lf_lo.astype(F32)).astype(BF16)
    B = _dot(tril, lf_hi) + (_dot(tril, lf_lo) + _dot(tril, lf_lo2))
    GT = G.T
    BT = B.T

    for h in range(NH_M):
        cs = slice(h * DH_M, (h + 1) * DH_M)
        b_col = B[:, S_FP + h:S_FP + h + 1]
        ig_col = G[:, S_IP + h:S_IP + h + 1]
        b_row = BT[S_FP + h:S_FP + h + 1, :]
        ig_row = GT[S_IP + h:S_IP + h + 1, :]
        g = B[L - 1:L, S_FP + h:S_FP + h + 1]
        m_prev = m_st[h:h + 1, 0:1]

        dmat = jnp.where(causal, b_col - b_row + ig_row, -jnp.inf)
        m_intra = jnp.max(dmat, axis=-1, keepdims=True)
        a_col = g - b_col + ig_col
        m_loc = jnp.max(a_col, axis=0, keepdims=True)
        w_loc = jnp.exp(a_col - m_loc)

        qh = qs[:, cs]
        kh = ks[:, cs]
        vh = v_ref[:, cs]
        qb = qh.astype(BF16)
        kb = kh.astype(BF16)
        vb = vh.astype(BF16)
        c_prev = c_st[h]
        n_prev = n_st[h:h + 1, :]

        inter = b_col + m_prev
        m_t = jnp.maximum(inter, m_intra)
        s_inter = jnp.exp(inter - m_t)
        w_intra = jnp.exp(dmat - m_t) * _dot_nt(qb, kb)
        num = _dot(w_intra.astype(BF16), vb) + s_inter * _dot(qb, c_prev.astype(BF16))
        den = (jnp.sum(w_intra, axis=-1, keepdims=True)
               + s_inter * jnp.sum(qh * n_prev, axis=-1, keepdims=True))
        hh = num / jnp.maximum(jnp.abs(den), jnp.exp(-m_t))
        hn = hh * lax.rsqrt(jnp.mean(hh * hh, axis=-1, keepdims=True) + EPS)
        y_ref[:, cs] = (hn * nw_ref[:, cs] * _sigmoid(o_ref[:, cs])).astype(y_ref.dtype)

        kw = kh * w_loc
        c_loc = _dot_tn(kw.astype(BF16), vb)
        n_loc = jnp.sum(kw, axis=0, keepdims=True)
        m_new = jnp.maximum(g + m_prev, m_loc)
        s_p = jnp.exp(g + m_prev - m_new)
        s_l = jnp.exp(m_loc - m_new)
        c_st[h] = s_p * c_prev + s_l * c_loc
        n_st[h:h + 1, :] = s_p * n_prev + s_l * n_loc
        m_st[h:h + 1, :] = jnp.broadcast_to(m_new, (1, 128))


def _mlstm_call(z, zs, conv_w, bg_row, nw_row):
    S = z.shape[0]
    W = NH_M * DH_M
    L = CHUNK
    row = lambda c: (0, 0)
    return pl.pallas_call(
        _mlstm_kernel,
        grid=(S // L,),
        in_specs=[
            pl.BlockSpec((L, W), lambda c: (c, C_Q // W)),
            pl.BlockSpec((L, W), lambda c: (c, C_K // W)),
            pl.BlockSpec((L, W), lambda c: (c, C_V // W)),
            pl.BlockSpec((L, W), lambda c: (c, C_O // W)),
            pl.BlockSpec((L, 128), lambda c: (c, 0)),
            pl.BlockSpec((CONV_K, 2 * W), row),
            pl.BlockSpec((1, 128), row),
            pl.BlockSpec((1, W), row),
        ],
        out_specs=pl.BlockSpec((L, W), lambda c: (c, 0)),
        out_shape=jax.ShapeDtypeStruct((S, W), BF16),
        scratch_shapes=[
            pltpu.VMEM((8, W), F32), pltpu.VMEM((8, W), F32),
            pltpu.VMEM((L, W), F32), pltpu.VMEM((L, W), F32),
            pltpu.VMEM((NH_M, DH_M, DH_M), F32),
            pltpu.VMEM((NH_M, DH_M), F32),
            pltpu.VMEM((NH_M, 128), F32),
        ],
        compiler_params=_cparams(("arbitrary",)),
        name="mlstm",
    )(z, z, z, z, zs, conv_w, bg_row, nw_row)


def _kvprep_kernel(ka_ref, va_ref, zs_ref, kn_ref, kan_ref, vta_ref, kie_ref, kio_ref):
    tm = ka_ref.shape[0]
    for n in range(NKV_A):
        cs = slice(n * DH_A, (n + 1) * DH_A)
        k = ka_ref[:, cs]
        r = lax.rsqrt(jnp.mean(k * k, axis=-1, keepdims=True) + EPS)
        kan_ref[:, cs] = ((k * r) * kn_ref[...]).astype(BF16)
        vta_ref[n * VT_ROWS:n * VT_ROWS + DH_A, :] = va_ref[:, cs].T.astype(BF16)
        vta_ref[n * VT_ROWS + DH_A:(n + 1) * VT_ROWS, :] = jnp.ones((VT_ROWS - DH_A, tm), BF16)
    zs = zs_ref[...]
    lane = lax.broadcasted_iota(I32, zs.shape, 1)
    kie_ref[...] = jnp.where(lane < DI, zs, 0.0).astype(BF16)
    kio_ref[...] = jnp.where(lane >= DI, pltpu.roll(zs, DI, 1), 0.0).astype(BF16)


def _kvprep_call(z, zs, kn_row):
    S = z.shape[0]
    tm = 512
    W = NKV_A * DH_A
    return pl.pallas_call(
        _kvprep_kernel,
        grid=(S // tm,),
        in_specs=[
            pl.BlockSpec((tm, W), lambda m: (m, C_KA // W)),
            pl.BlockSpec((tm, W), lambda m: (m, C_VA // W)),
            pl.BlockSpec((tm, 128), lambda m: (m, 0)),
            pl.BlockSpec((1, DH_A), lambda m: (0, 0)),
        ],
        out_specs=[
            pl.BlockSpec((tm, W), lambda m: (m, 0)),
            pl.BlockSpec((NKV_A * VT_ROWS, tm), lambda m: (0, m)),
            pl.BlockSpec((tm, 128), lambda m: (m, 0)),
            pl.BlockSpec((tm, 128), lambda m: (m, 0)),
        ],
        out_shape=[
            jax.ShapeDtypeStruct((S, W), BF16),
            jax.ShapeDtypeStruct((NKV_A * VT_ROWS, S), BF16),
            jax.ShapeDtypeStruct((S, 128), BF16),
            jax.ShapeDtypeStruct((S, 128), BF16),
        ],
        compiler_params=_cparams(("arbitrary",)),
        name="kv_prep",
    )(z, z, zs, kn_row)


def _sortable(x):
    b = pltpu.bitcast(x, I32)
    return jnp.where(b < 0, b ^ jnp.int32(0x7FFFFFFF), b)


def _dsa_kernel(topk, qi_ref, qa_ref, zs_ref, kie_ref, kio_ref, kan_ref, vta_ref, qn_ref,
                tz_ref, bfar_ref, y_ref,
                skey, qst, qist, thr_s, jcut_s, m_s, acc_s, lg_s):
    i = pl.program_id(0)
    S = skey.shape[0]
    R = KBLK // QBLK
    nkb = i // R + 1
    nfar = jnp.maximum((i - 1) // R, 0)
    PK = 16
    keyi = lax.broadcasted_iota(I32, (KBLK, QBLK), 0)
    qcol = lax.broadcasted_iota(I32, (KBLK, QBLK), 1) + i * QBLK

    for p in range(NH_I // 2):
        qist[p * QBLK:(p + 1) * QBLK, :] = qi_ref[:, p * 128:(p + 1) * 128].astype(BF16)
    wT = (zs_ref[...] * ((NH_I * DI) ** -0.5)).T
    for h in range(NH_A):
        q = qa_ref[:, h * DH_A:(h + 1) * DH_A]
        r = lax.rsqrt(jnp.mean(q * q, axis=-1, keepdims=True) + EPS)
        qst[h * QBLK:(h + 1) * QBLK, :] = ((q * r) * qn_ref[...] * QSCALE).astype(BF16)

    def score_block(jb, carry):
        k0 = pl.multiple_of(jb * KBLK, KBLK)
        ke = kie_ref[pl.ds(k0, KBLK), :]
        ko = kio_ref[pl.ds(k0, KBLK), :]

        def pair_scores(c):
            q = qist[c * 2 * QBLK:(c + 1) * 2 * QBLK, :]
            return _dot_nt(ke, q), _dot_nt(ko, q)

        sc = None
        nxt = pair_scores(0)
        for c in range(NH_I // 4):
            se, so = nxt
            if c + 1 < NH_I // 4:
                nxt = pair_scores(c + 1)
            for pp in range(2):
                p = 2 * c + pp
                te = jnp.maximum(se[:, pp * QBLK:(pp + 1) * QBLK], 0.0) * wT[S_WI + 2 * p:S_WI + 2 * p + 1, :]
                to = jnp.maximum(so[:, pp * QBLK:(pp + 1) * QBLK], 0.0) * wT[S_WI + 2 * p + 1:S_WI + 2 * p + 2, :]
                sc = te + to if sc is None else sc + te + to
        sc = jnp.where(sc == 0.0, 0.0, sc)
        valid = (keyi + k0) <= qcol
        skey[pl.ds(k0, KBLK), :] = jnp.where(valid, _sortable(sc), jnp.int32(INT_MIN))
        return carry

    lax.fori_loop(0, nkb, score_block, 0)

    def count(pred_fn):
        def body(jb, cnt):
            k0 = pl.multiple_of(jb * KBLK, KBLK)
            hit = jnp.where(pred_fn(skey[pl.ds(k0, KBLK), :], k0), 1, 0)
            return cnt + jnp.sum(hit.reshape(KBLK // 8, 8, QBLK), axis=0)
        cnt = lax.fori_loop(0, nkb, body, jnp.zeros((8, QBLK), I32))
        return jnp.sum(cnt, axis=0, keepdims=True)

    def bit_pass(b, cand):
        trial = cand ^ jnp.left_shift(jnp.int32(1), 31 - b)
        cnt = count(lambda t, _: t >= trial)
        return jnp.where(cnt >= topk, trial, cand)

    thr = lax.fori_loop(0, 32, bit_pass, jnp.full((1, QBLK), INT_MIN, I32))
    c_gt = count(lambda t, _: t > thr)
    c_eq = count(lambda t, _: t == thr)
    need = topk - c_gt
    short = thr == INT_MIN
    thr_s[...] = jnp.broadcast_to(jnp.where(short, jnp.int32(INT_MIN + 1), thr), (8, QBLK))
    jcut_s[...] = jnp.full((8, QBLK), S, I32)
    tie = jnp.logical_and(jnp.logical_not(short), c_eq > need)

    @pl.when(jnp.max(jnp.where(tie, 1, 0)) > 0)
    def _():
        def idx_pass(b, pos):
            t = pos + jnp.left_shift(jnp.int32(1), 13 - b)
            cnt = count(lambda tl, k0: jnp.logical_and(tl == thr, (keyi + k0) < t))
            return jnp.where(cnt < need, t, pos)
        pos = lax.fori_loop(0, 14, idx_pass, jnp.zeros((1, QBLK), I32))
        jcut_s[...] = jnp.broadcast_to(jnp.where(tie, pos, S), (8, QBLK))

    m_s[...] = jnp.full(m_s.shape, NEG, F32)
    acc_s[...] = jnp.zeros_like(acc_s)

    def logits(kstart, n):
        kn = kan_ref[pl.ds(kstart, KBLK), n * DH_A:(n + 1) * DH_A]
        return _dot_nt(kn, qst[n * GRP_A * QBLK:(n + 1) * GRP_A * QBLK, :]).astype(BF16)

    def attend(jb, last, near):
        k0 = pl.multiple_of(jb * KBLK, KBLK)
        thr_r = thr_s[0:1, :]
        jc_r = jcut_s[0:1, :]
        sk = skey[pl.ds(k0, KBLK), :]
        sel = jnp.logical_or(sk > thr_r,
                             jnp.logical_and(sk == thr_r, (keyi + k0) <= jc_r))
        negm = jnp.where(sel, 0.0, NEG).astype(BF16).reshape(KBLK // PK, PK, QBLK)
        if near:
            off = pl.multiple_of(KBLK - (i * QBLK - k0), QBLK)

        def col_max(x3):
            n = KBLK // PK
            while n > 1:
                n //= 2
                x3 = jnp.maximum(x3[:n], x3[n:])
            return jnp.max(x3[0].astype(F32), axis=0, keepdims=True)

        def packed_row(r):
            return jnp.broadcast_to(r, (PK, QBLK)).astype(BF16)[None]

        lg_next = lg_s[...]
        for n in range(NKV_A):
            lg = lg_next
            if n + 1 < NKV_A:
                lg_next = logits(k0, n + 1)
            else:
                k1 = pl.multiple_of(jnp.minimum(jb + 1, last) * KBLK, KBLK)
                lg_s[...] = logits(k1, 0)
            ps, alphas = [], []
            for g in range(GRP_A):
                h = n * GRP_A + g
                x = lg[:, g * QBLK:(g + 1) * QBLK].reshape(KBLK // PK, PK, QBLK) + negm
                m_old = m_s[h:h + 1, :]
                if near:
                    x = x + tz_ref[h, pl.ds(off, KBLK), :].reshape(KBLK // PK, PK, QBLK)
                    shift = packed_row(jnp.maximum(m_old, col_max(x)))
                    m_new = shift[0, 0:1, :].astype(F32)
                else:
                    bf = bfar_ref[h:h + 1, :]
                    shift = packed_row(jnp.maximum(m_old, col_max(x) + bf) - bf)
                    m_new = shift[0, 0:1, :].astype(F32) + bf
                p = jnp.exp2(x - shift)
                m_s[h:h + 1, :] = m_new
                alphas.append(jnp.exp2(m_old - m_new))
                ps.append(p.reshape(KBLK, QBLK))
            pt = jnp.concatenate(ps, axis=1)
            pv = _dot(vta_ref[n * VT_ROWS:(n + 1) * VT_ROWS, pl.ds(k0, KBLK)], pt)
            acc_s[n] = jnp.concatenate(alphas, axis=1) * acc_s[n] + pv

    def far_body(jb, carry):
        attend(jb, nkb - 1, False)
        return carry

    def near_body(jb, carry):
        attend(jb, nkb - 1, True)
        return carry

    lg_s[...] = logits(0, 0)
    lax.fori_loop(0, nfar, far_body, 0)
    lax.fori_loop(nfar, nkb, near_body, 0)

    for h in range(NH_A):
        n, g = divmod(h, GRP_A)
        num = acc_s[n, 0:DH_A, g * QBLK:(g + 1) * QBLK]
        den = acc_s[n, DH_A:DH_A + 1, g * QBLK:(g + 1) * QBLK]
        y_ref[:, h * DH_A:(h + 1) * DH_A] = (num / den).T.astype(y_ref.dtype)


def _dsa_call(z, zs, kie, kio, kan, vta, qn_row, tz, bfar):
    S = z.shape[0]
    topk = min(TOPK_MAX, S // 4)
    WQ = NH_A * DH_A
    WI = NH_I * DI
    res = dict(pipeline_mode=pl.Buffered(1))
    return pl.pallas_call(
        functools.partial(_dsa_kernel, topk),
        grid=(S // QBLK,),
        in_specs=[
            pl.BlockSpec((QBLK, WI), lambda i: (i, C_QI // WI)),
            pl.BlockSpec((QBLK, WQ), lambda i: (i, C_QA // WQ)),
            pl.BlockSpec((QBLK, 128), lambda i: (i, 0)),
            pl.BlockSpec((S, 128), lambda i: (0, 0), **res),
            pl.BlockSpec((S, 128), lambda i: (0, 0), **res),
            pl.BlockSpec((S, NKV_A * DH_A), lambda i: (0, 0), **res),
            pl.BlockSpec((NKV_A * VT_ROWS, S), lambda i: (0, 0), **res),
            pl.BlockSpec((1, DH_A), lambda i: (0, 0)),
            pl.BlockSpec((NH_A, 2 * KBLK, QBLK), lambda i: (0, 0, 0), **res),
            pl.BlockSpec((NH_A, QBLK), lambda i: (0, 0)),
        ],
        out_specs=pl.BlockSpec((QBLK, WQ), lambda i: (i, 0)),
        out_shape=jax.ShapeDtypeStruct((S, WQ), BF16),
        scratch_shapes=[
            pltpu.VMEM((S, QBLK), I32),
            pltpu.VMEM((NH_A * QBLK, DH_A), BF16),
            pltpu.VMEM((NH_I // 2 * QBLK, 128), BF16),
            pltpu.VMEM((8, QBLK), I32),
            pltpu.VMEM((8, QBLK), I32),
            pltpu.VMEM((NH_A, QBLK), F32),
            pltpu.VMEM((NKV_A, VT_ROWS, GRP_A * QBLK), F32),
            pltpu.VMEM((KBLK, GRP_A * QBLK), BF16),
        ],
        compiler_params=_cparams(("arbitrary",)),
        name="dsa",
    )(z, z, zs, kie, kio, kan, vta, qn_row, tz, bfar)


def _t5_bucket(dist):
    n = jnp.maximum(dist, 0)
    max_exact = N_BUCKETS // 2
    nf = jnp.maximum(n, max_exact).astype(F32)
    large = max_exact + (jnp.log(nf / max_exact) / math.log(MAX_DIST / max_exact)
                         * (N_BUCKETS - max_exact)).astype(I32)
    large = jnp.minimum(large, N_BUCKETS - 1)
    return jnp.where(n < max_exact, n, large)


def _bias_tables(rel_bias):
    cc = jnp.arange(2 * KBLK, dtype=I32)[:, None]
    qq = jnp.arange(QBLK, dtype=I32)[None, :]
    bucket = _t5_bucket(qq - cc + KBLK)
    onehot = (bucket[None] == jnp.arange(N_BUCKETS, dtype=I32)[:, None, None]).astype(F32)
    rb = rel_bias.astype(F32) * LOG2E
    tz = jnp.einsum('bcq,bh->hcq', onehot, rb, precision=lax.Precision.HIGHEST)
    bfar = jnp.broadcast_to(rb[N_BUCKETS - 1][:, None], (NH_A, QBLK))
    return tz.astype(BF16), bfar


def _merge_kernel(ym_ref, ya_ref, wm_ref, wa_ref, gm_ref, ga_ref, o_ref):
    a = _dot(ym_ref[...], wm_ref[0])
    b = _dot(ya_ref[...], wa_ref[0])
    o_ref[...] = (_sigmoid(gm_ref[...]) * a + _sigmoid(ga_ref[...]) * b).astype(o_ref.dtype)


def _merge_call(l, ym, ya, wpm, wpa, z):
    S = ym.shape[0]
    tm, tn = 1024, 512
    D = D_MODEL
    return pl.pallas_call(
        _merge_kernel,
        grid=(S // tm, D // tn),
        in_specs=[
            pl.BlockSpec((tm, D), lambda m, n: (m, 0)),
            pl.BlockSpec((tm, D), lambda m, n: (m, 0)),
            pl.BlockSpec((1, D, tn), lambda m, n: (l, 0, n)),
            pl.BlockSpec((1, D, tn), lambda m, n: (l, 0, n)),
            pl.BlockSpec((tm, tn), lambda m, n: (m, C_GM // tn + n)),
            pl.BlockSpec((tm, tn), lambda m, n: (m, C_GA // tn + n)),
        ],
        out_specs=pl.BlockSpec((tm, tn), lambda m, n: (m, n)),
        out_shape=jax.ShapeDtypeStruct((S, D), BF16),
        compiler_params=_cparams(("arbitrary", "arbitrary")),
        name="merge",
    )(ym, ya, wpm, wpa, z, z)


def _out_kernel(mg_ref, wo_ref, x_ref, g1_ref, nw_ref, sc_ref, sh_ref, wr_ref, br_ref,
                xo_ref, h2_ref, lg_ref):
    xn = x_ref[...] + g1_ref[...] * _dot(mg_ref[...], wo_ref[0])
    xo_ref[...] = xn
    r = lax.rsqrt(jnp.mean(xn * xn, axis=-1, keepdims=True) + EPS)
    h = (xn * r) * nw_ref[...] * (1.0 + sc_ref[...]) + sh_ref[...]
    h2_ref[...] = _pack_halves(h)
    lg_ref[...] = _dot3(h, wr_ref[...]) + br_ref[...]


def _out_call(l, merged, wo, x2, g1, nw, sc, sh, w_router, b_router):
    S = x2.shape[0]
    tm = 256
    D = D_MODEL
    row = lambda m: (0, 0)
    return pl.pallas_call(
        _out_kernel,
        grid=(S // tm,),
        in_specs=[
            pl.BlockSpec((tm, D), lambda m: (m, 0)),
            pl.BlockSpec((1, D, D), lambda m: (l, 0, 0), pipeline_mode=pl.Buffered(1)),
            pl.BlockSpec((tm, D), lambda m: (m, 0)),
            pl.BlockSpec((1, D), row), pl.BlockSpec((1, D), row),
            pl.BlockSpec((1, D), row), pl.BlockSpec((1, D), row),
            pl.BlockSpec((D, 128), row),
            pl.BlockSpec((1, 128), row),
        ],
        out_specs=[
            pl.BlockSpec((tm, D), lambda m: (m, 0)),
            pl.BlockSpec((tm, D // 2), lambda m: (m, 0)),
            pl.BlockSpec((tm, 128), lambda m: (m, 0)),
        ],
        out_shape=[
            jax.ShapeDtypeStruct((S, D), F32),
            jax.ShapeDtypeStruct((S, D // 2), I32),
            jax.ShapeDtypeStruct((S, 128), F32),
        ],
        compiler_params=_cparams(("arbitrary",)),
        name="out_proj",
    )(merged, wo, x2, g1, nw, sc, sh, w_router, b_router)


def _pack_halves(h):
    n = h.shape[1] // 2
    bits = pltpu.bitcast(h.astype(BF16).astype(F32), I32)
    lo = lax.shift_right_logical(bits[:, :n], jnp.int32(16))
    hi = bits[:, n:] & jnp.int32(-65536)
    return hi | lo


def _unpack_halves(w):
    lo = pltpu.bitcast(lax.shift_left(w, jnp.int32(16)), F32)
    hi = pltpu.bitcast(w & jnp.int32(-65536), F32)
    return jnp.concatenate([lo, hi], axis=1).astype(BF16)


def _moe_kernel(l, be_ref, nu_ref, first_ref, slot_ref, nxt_ref, xs_ref, w1_hbm, w3_hbm, w2_hbm, o_ref,
                w1s, w3s, w2s, w1b, w3b, w2b, sem):
    b = pl.program_id(0)

    def weight_copies(e, s):
        return (pltpu.make_async_copy(w1_hbm.at[l, e], w1s.at[s], sem.at[s, 0]),
                pltpu.make_async_copy(w3_hbm.at[l, e], w3s.at[s], sem.at[s, 1]),
                pltpu.make_async_copy(w2_hbm.at[l, e], w2s.at[s], sem.at[s, 2]))

    @pl.when(jnp.logical_and(b < nu_ref[0], first_ref[b] == 1))
    def _():
        s = slot_ref[b]
        e = be_ref[b]

        @pl.when(b == 0)
        def _():
            for cp in weight_copies(e, s):
                cp.start()

        for cp in weight_copies(e, s):
            cp.wait()

        @pl.when(nxt_ref[b] >= 0)
        def _():
            for cp in weight_copies(nxt_ref[b], 1 - s):
                cp.start()

        w1b[...] = w1s[s].astype(BF16)
        w3b[...] = w3s[s].astype(BF16)
        w2b[...] = w2s[s].astype(BF16)

    @pl.when(b < nu_ref[0])
    def _():
        xb = _unpack_halves(xs_ref[...])
        a = _dot(xb, w1b[...])
        c = _dot(xb, w3b[...])
        hmid = (a * _sigmoid(a)) * c
        o_ref[...] = _dot(hmid.astype(BF16), w2b[...])

    @pl.when(b >= nu_ref[0])
    def _():
        o_ref[...] = jnp.zeros_like(o_ref)


def _combine_kernel(x_ref, y0_ref, y1_ref, gate_ref, g2_ref, *rest):
    g = gate_ref[...]
    y = y0_ref[...] * g[:, 0:1] + y1_ref[...] * g[:, 1:2]
    xo = x_ref[...] + g2_ref[...] * y
    if len(rest) == 1:
        rest[0][...] = xo
    else:
        nw_ref, sc_ref, sh_ref, ws_ref, o_ref, h_ref, zs_ref = rest
        o_ref[...] = xo
        _mixer_input(xo, nw_ref, sc_ref, sh_ref, ws_ref, h_ref, zs_ref)


def _combine_call(xn, y0, y1, gate, g2, nxt=None):
    S, D = xn.shape
    tm = 512
    blk = pl.BlockSpec((tm, D), lambda m: (m, 0))
    row = pl.BlockSpec((1, D), lambda m: (0, 0))
    in_specs = [blk, blk, blk, pl.BlockSpec((tm, TOP_K), lambda m: (m, 0)), row]
    args = [xn, y0, y1, gate, g2]
    out_specs, out_shape = blk, jax.ShapeDtypeStruct((S, D), F32)
    if nxt is not None:
        ln, nw, sc, sh, w_small = nxt
        in_specs += [row, row, row, pl.BlockSpec((1, D, 128), lambda m: (ln, 0, 0))]
        args += [nw, sc, sh, w_small]
        out_specs = [blk, blk, pl.BlockSpec((tm, 128), lambda m: (m, 0))]
        out_shape = [out_shape, jax.ShapeDtypeStruct((S, D), BF16), jax.ShapeDtypeStruct((S, 128), F32)]
    return pl.pallas_call(
        _combine_kernel,
        grid=(S // tm,),
        in_specs=in_specs,
        out_specs=out_specs,
        out_shape=out_shape,
        compiler_params=_cparams(("arbitrary",)),
        name="moe_combine",
    )(*args)


def _moe_call(l, plan, xs, w1, w3, w2):
    rows = xs.shape[0]
    nblk = rows // MOE_BLK
    D = D_MODEL
    hbm = pl.BlockSpec(memory_space=pl.ANY)
    grid_spec = pltpu.PrefetchScalarGridSpec(
        num_scalar_prefetch=len(plan),
        grid=(nblk,),
        in_specs=[pl.BlockSpec((MOE_BLK, D // 2), lambda b, *_: (b, 0)), hbm, hbm, hbm],
        out_specs=pl.BlockSpec((MOE_BLK, D), lambda b, *_: (b, 0)),
        scratch_shapes=[pltpu.VMEM((2, D, D_FF_E), F32), pltpu.VMEM((2, D, D_FF_E), F32),
                        pltpu.VMEM((2, D_FF_E, D), F32),
                        pltpu.VMEM((D, D_FF_E), BF16), pltpu.VMEM((D, D_FF_E), BF16),
                        pltpu.VMEM((D_FF_E, D), BF16),
                        pltpu.SemaphoreType.DMA((2, 3))],
    )
    return pl.pallas_call(
        functools.partial(_moe_kernel, l),
        grid_spec=grid_spec,
        out_shape=jax.ShapeDtypeStruct((rows, D), F32),
        compiler_params=_cparams(("arbitrary",)),
        name="moe_ffn",
    )(*plan, xs, w1, w3, w2)


def _route(logits):
    n = logits.shape[0]
    g_logit = logits[:, :N_GROUPS]
    g_sel = jnp.argmax(g_logit, axis=-1)
    g_oh = g_sel[:, None] == jnp.arange(N_GROUPS)[None, :]
    p_grp = jnp.sum(jnp.where(g_oh, jax.nn.softmax(g_logit, axis=-1), 0.0), axis=-1, keepdims=True)
    e_logit = logits[:, N_GROUPS:N_GROUPS + N_EXPERTS].reshape(n, N_GROUPS, EXP_PER_GROUP)
    e_logit = jnp.sum(jnp.where(g_oh[:, :, None], e_logit, 0.0), axis=1)
    top_v, top_i = lax.top_k(e_logit, TOP_K)
    gate = p_grp * jax.nn.softmax(top_v, axis=-1)
    expert = (g_sel[:, None] * EXP_PER_GROUP + top_i).reshape(-1).astype(I32)
    m = n * TOP_K
    onehot = (expert[:, None] == jnp.arange(N_EXPERTS, dtype=I32)[None, :]).astype(I32)
    rank = jnp.sum((jnp.cumsum(onehot, axis=0) - onehot) * onehot, axis=1)
    counts = jnp.sum(onehot, axis=0)
    pcounts = (counts + MOE_BLK - 1) // MOE_BLK * MOE_BLK
    pends = jnp.cumsum(pcounts)
    dest = (pends - pcounts)[expert] + rank
    nblk = -(-m // MOE_BLK) + N_EXPERTS
    rows = nblk * MOE_BLK
    tok = jnp.arange(m, dtype=I32) // TOP_K
    row_tok = (jnp.arange(rows, dtype=I32) % n).at[dest].set(tok)
    blk_start = jnp.arange(nblk, dtype=I32)[:, None] * MOE_BLK
    blk_exp = jnp.minimum(jnp.sum((pends[None, :] <= blk_start).astype(I32), axis=1),
                          N_EXPERTS - 1).astype(I32)
    nused = (pends[-1] // MOE_BLK).astype(I32).reshape(1)
    prev = jnp.concatenate([jnp.full((1,), -1, I32), blk_exp[:-1]])
    first = (blk_exp != prev).astype(I32)
    slot = (jnp.cumsum(first) - 1) & 1
    eid = jnp.arange(N_EXPERTS, dtype=I32)
    later = jnp.logical_and(eid[None, :] > eid[:, None], (counts > 0)[None, :])
    nxt_e = jnp.min(jnp.where(later, eid[None, :], N_EXPERTS), axis=1)
    nxt = jnp.where(nxt_e < N_EXPERTS, nxt_e, -1)[blk_exp].astype(I32)
    plan = (blk_exp, nused, first, slot.astype(I32), nxt)
    return row_tok, gate, plan, dest.reshape(n, TOP_K)


def _layer(l, x2, h, zs, mod, nxt, norm2, w_big, conv_w, bg_row, mlstm_norm, q_norm, k_norm,
           tz, bfar, wpm, wpa, wo, w_router, b_router, w1, w3, w2):
    D = D_MODEL
    sh1, sc1, g1, sh2, sc2, g2 = [mod[:, j * D:(j + 1) * D] for j in range(6)]
    z = _in_call(l, h, w_big)
    ym = _mlstm_call(z, zs, conv_w, bg_row, mlstm_norm[None, :])
    kan, vta, kie, kio = _kvprep_call(z, zs, k_norm[None, :])
    ya = _dsa_call(z, zs, kie, kio, kan, vta, q_norm[None, :], tz, bfar)
    merged = _merge_call(l, ym, ya, wpm, wpa, z)
    xn, h2, logits = _out_call(l, merged, wo, x2, g1, norm2[None, :], sc2, sh2, w_router, b_router)
    row_tok, gate, plan, dest = _route(logits)
    xs = h2[row_tok]
    yb = _moe_call(l, plan, xs, w1, w3, w2)
    return _combine_call(xn, yb[dest[:, 0]], yb[dest[:, 1]], gate, g2, nxt)


def kernel(x, c, w_ada, b_ada, norm1, norm2, w_in, conv_w, b_gate, mlstm_norm, q_norm, k_norm,
           rel_bias, w_proj_m, w_proj_a, w_out, w_grp, b_grp, w_exp, b_exp, w1, w3, w2):
    B, S, D = x.shape
    assert B == 1 and D == D_MODEL
    x2 = x.reshape(S, D)
    c16 = jnp.pad(c, ((0, 16 - B), (0, 0)))
    mods = _ada_call(c16, w_ada, b_ada[:, None, :])
    tz, bfar = _bias_tables(rel_bias)
    w_big, w_small = _wprep_call(jnp.swapaxes(w_in, 1, 2))
    wpm = w_proj_m.astype(BF16)
    wpa = w_proj_a.astype(BF16)
    wo = w_out.astype(BF16)

    def norm1_params(l):
        return (l, norm1[l][None, :], mods[l, 0:1, D:2 * D], mods[l, 0:1, 0:D], w_small)

    first = norm1_params(0)
    h, zs = _prenorm_call(first[0], x2, *first[1:])
    for l in range(DEPTH):
        bg_row = jnp.zeros((1, 128), F32).at[0, S_IP:S_IP + 2 * NH_M].set(b_gate[l])
        w_router = jnp.concatenate([w_grp[l], w_exp[l],
                                    jnp.zeros((D, 128 - N_GROUPS - N_EXPERTS), F32)], axis=1)
        b_router = jnp.concatenate([b_grp[l], b_exp[l],
                                    jnp.zeros((128 - N_GROUPS - N_EXPERTS,), F32)])[None, :]
        nxt = norm1_params(l + 1) if l + 1 < DEPTH else None
        out = _layer(l, x2, h, zs, mods[l, 0:1, :], nxt, norm2[l], w_big, conv_w[l], bg_row,
                     mlstm_norm[l], q_norm[l], k_norm[l], tz, bfar, wpm, wpa, wo,
                     w_router, b_router, w1, w3, w2)
        if nxt is None:
            x2 = out
        else:
            x2, h, zs = out
    return x2.reshape(B, S, D)
```

```python
import functools
import math

import numpy as np
import jax
import jax.numpy as jnp
from jax import lax
from jax.experimental import pallas as pl
from jax.experimental.pallas import tpu as pltpu

F32 = jnp.float32
BF16 = jnp.bfloat16
I32 = jnp.int32

D_MODEL = 2048
DEPTH = 4
NH_M = 8
DH_M = 256
CONV_K = 4
CHUNK = 128
DH_A = 128
NH_A = 16
NKV_A = 4
GRP_A = NH_A // NKV_A
NH_I = 16
DI = 64
TOPK_MAX = 256
QBLK = 128
KBLK = 512
N_BUCKETS = 32
MAX_DIST = 128
N_GROUPS = 4
EXP_PER_GROUP = 8
N_EXPERTS = 32
TOP_K = 2
D_FF_E = 512
MOE_BLK = 128
EPS = 1e-6

IN_SIZES = (2 * NH_M * DH_M, NH_M * DH_M, NH_M * DH_M, NH_M, NH_M, NH_A * DH_A, NKV_A * DH_A,
            NKV_A * DH_A, NH_I * DI, DI, NH_I, D_MODEL, D_MODEL)
D_IN = sum(IN_SIZES)
C_Q, C_K, C_V, C_O = 0, 2048, 4096, 6144
C_QA, C_KA, C_VA, C_QI, C_GM, C_GA = 8192, 10240, 10752, 11264, 12288, 14336
N_BIG = 16384
S_KI, S_WI, S_IP, S_FP = 0, 64, 80, 88

INT_MIN = -(2 ** 31)
NEG = -(2.0 ** 100)
LOG2E = math.log2(math.e)
QSCALE = DH_A ** -0.5 * LOG2E
VT_ROWS = DH_A + 16
VMEM_LIMIT = 56 * 1024 * 1024


def _cparams(sem):
    return pltpu.CompilerParams(dimension_semantics=sem, vmem_limit_bytes=VMEM_LIMIT)


def _split(a):
    hi = a.astype(BF16)
    lo = (a - hi.astype(F32)).astype(BF16)
    return hi, lo


def _dot(a, b):
    return jnp.dot(a, b, preferred_element_type=F32)


def _dot3(a, b):
    ah, al = _split(a)
    bh, bl = _split(b)
    return _dot(ah, bh) + (_dot(al, bh) + _dot(ah, bl))


def _dot_nt(a, b):
    return lax.dot_general(a, b, (((1,), (1,)), ((), ())), preferred_element_type=F32)


def _dot_tn(a, b):
    return lax.dot_general(a, b, (((0,), (0,)), ((), ())), preferred_element_type=F32)


def _sigmoid(x):
    return 1.0 / (1.0 + jnp.exp(-x))


def _ada_kernel(c_ref, w_ref, b_ref, o_ref):
    c = c_ref[...]
    ca = c * _sigmoid(c)
    o_ref[0] = _dot3(ca, w_ref[0]) + b_ref[0]


def _ada_call(c16, w_ada, b_ada3):
    tn = 1024
    n6 = w_ada.shape[-1]
    return pl.pallas_call(
        _ada_kernel,
        grid=(DEPTH, n6 // tn),
        in_specs=[
            pl.BlockSpec((16, D_MODEL), lambda l, n: (0, 0)),
            pl.BlockSpec((1, D_MODEL, tn), lambda l, n: (l, 0, n)),
            pl.BlockSpec((1, 1, tn), lambda l, n: (l, 0, n)),
        ],
        out_specs=pl.BlockSpec((1, 16, tn), lambda l, n: (l, 0, n)),
        out_shape=jax.ShapeDtypeStruct((DEPTH, 16, n6), F32),
        compiler_params=_cparams(("arbitrary", "arbitrary")),
        name="adaln",
    )(c16, w_ada, b_ada3)


def _wprep_kernel(w_ref, wb_ref, ws_ref):
    x = w_ref[0]
    cuts = [0] + [int(t) for t in np.cumsum(IN_SIZES)]
    seg = lambda a, b: x[cuts[a]:cuts[b], :]
    wb_ref[0] = jnp.concatenate([seg(0, 3), seg(5, 9), seg(11, 13)], axis=0).T.astype(BF16)
    ws_ref[0] = jnp.concatenate([seg(9, 11), seg(3, 5), jnp.zeros((32, x.shape[1]), F32)], axis=0).T


def _wprep_call(w_in_t):
    tm = 128
    return pl.pallas_call(
        _wprep_kernel,
        grid=(DEPTH, D_MODEL // tm),
        in_specs=[pl.BlockSpec((1, D_IN, tm), lambda l, m: (l, 0, m))],
        out_specs=[
            pl.BlockSpec((1, tm, N_BIG), lambda l, m: (l, m, 0)),
            pl.BlockSpec((1, tm, 128), lambda l, m: (l, m, 0)),
        ],
        out_shape=[
            jax.ShapeDtypeStruct((DEPTH, D_MODEL, N_BIG), BF16),
            jax.ShapeDtypeStruct((DEPTH, D_MODEL, 128), F32),
        ],
        compiler_params=_cparams(("arbitrary", "arbitrary")),
        name="w_prep",
    )(w_in_t)


def _mixer_input(x, nw_ref, sc_ref, sh_ref, ws_ref, h_ref, zs_ref):
    r = lax.rsqrt(jnp.mean(x * x, axis=-1, keepdims=True) + EPS)
    h = (x * r) * nw_ref[...] * (1.0 + sc_ref[...]) + sh_ref[...]
    h_ref[...] = h.astype(BF16)
    zs_ref[...] = _dot3(h, ws_ref[0])


def _prenorm_kernel(x_ref, nw_ref, sc_ref, sh_ref, ws_ref, h_ref, zs_ref):
    _mixer_input(x_ref[...], nw_ref, sc_ref, sh_ref, ws_ref, h_ref, zs_ref)


def _prenorm_call(l, x2, nw, sc, sh, w_small):
    S, D = x2.shape
    tm = 512
    row = lambda m: (0, 0)
    return pl.pallas_call(
        _prenorm_kernel,
        grid=(S // tm,),
        in_specs=[
            pl.BlockSpec((tm, D), lambda m: (m, 0)),
            pl.BlockSpec((1, D), row), pl.BlockSpec((1, D), row), pl.BlockSpec((1, D), row),
            pl.BlockSpec((1, D, 128), lambda m: (l, 0, 0)),
        ],
        out_specs=[pl.BlockSpec((tm, D), lambda m: (m, 0)), pl.BlockSpec((tm, 128), lambda m: (m, 0))],
        out_shape=[jax.ShapeDtypeStruct((S, D), BF16), jax.ShapeDtypeStruct((S, 128), F32)],
        compiler_params=_cparams(("arbitrary",)),
        name="prenorm",
    )(x2, nw, sc, sh, w_small)


def _in_kernel(h_ref, wb_ref, z_ref):
    z_ref[...] = _dot(h_ref[...], wb_ref[0])


def _in_call(l, h, w_big):
    S = h.shape[0]
    tm, tn = 1024, 1024
    return pl.pallas_call(
        _in_kernel,
        grid=(S // tm, N_BIG // tn),
        in_specs=[
            pl.BlockSpec((tm, D_MODEL), lambda m, n: (m, 0)),
            pl.BlockSpec((1, D_MODEL, tn), lambda m, n: (l, 0, n)),
        ],
        out_specs=pl.BlockSpec((tm, tn), lambda m, n: (m, n)),
        out_shape=jax.ShapeDtypeStruct((S, N_BIG), F32),
        compiler_params=_cparams(("arbitrary", "arbitrary")),
        name="in_proj",
    )(h, w_big)


def _mlstm_kernel(q_ref, k_ref, v_ref, o_ref, zs_ref, cw_ref, bg_ref, nw_ref, y_ref,
                  qprev, kprev, qs, ks, c_st, n_st, m_st):
    L = CHUNK

    @pl.when(pl.program_id(0) == 0)
    def _():
        qprev[...] = jnp.zeros_like(qprev)
        kprev[...] = jnp.zeros_like(kprev)
        c_st[...] = jnp.zeros_like(c_st)
        n_st[...] = jnp.zeros_like(n_st)
        m_st[...] = jnp.zeros_like(m_st)

    row8 = lax.broadcasted_iota(I32, (8, NH_M * DH_M), 0)

    def conv_silu(cur_ref, tail_ref, woff):
        cur = cur_ref[...]
        tail = tail_ref[...]
        out = None
        for j in range(CONV_K):
            s = CONV_K - 1 - j
            if s == 0:
                u = cur
            else:
                r = pltpu.roll(cur, s, 0)
                head = jnp.where(row8 < s, pltpu.roll(tail, s, 0), r[0:8])
                u = jnp.concatenate([head, r[8:]], axis=0)
            term = u * cw_ref[j:j + 1, woff:woff + NH_M * DH_M]
            out = term if out is None else out + term
        tail_ref[...] = cur[L - 8:L]
        return out * _sigmoid(out)

    qs[...] = conv_silu(q_ref, qprev, 0)
    ks[...] = conv_silu(k_ref, kprev, NH_M * DH_M) * (DH_M ** -0.5)

    G = zs_ref[...] + bg_ref[...]
    LF = jnp.minimum(G, 0.0) - jnp.log1p(jnp.exp(-jnp.abs(G)))
    ri = lax.broadcasted_iota(I32, (L, L), 0)
    ci = lax.broadcasted_iota(I32, (L, L), 1)
    causal = ci <= ri
    tril = jnp.where(causal, 1.0, 0.0).astype(BF16)
    lf_hi, lf_lo = _split(LF)
    lf_lo2 = (LF - lf_hi.astype(F32) - lf_lo.astype(F32)).astype(BF16)
    B = _dot(tril, lf_hi) + (_dot(tril, lf_lo) + _dot(tril, lf_lo2))
    GT = G.T
    BT = B.T

    for h in range(NH_M):
        cs = slice(h * DH_M, (h + 1) * DH_M)
        b_col = B[:, S_FP + h:S_FP + h + 1]
        ig_col = G[:, S_IP + h:S_IP + h + 1]
        b_row = BT[S_FP + h:S_FP + h + 1, :]
        ig_row = GT[S_IP + h:S_IP + h + 1, :]
        g = B[L - 1:L, S_FP + h:S_FP + h + 1]
        m_prev = m_st[h:h + 1, 0:1]

        dmat = jnp.where(causal, b_col - b_row + ig_row, -jnp.inf)
        m_intra = jnp.max(dmat, axis=-1, keepdims=True)
        a_col = g - b_col + ig_col
        m_loc = jnp.max(a_col, axis=0, keepdims=True)
        w_loc = jnp.exp(a_col - m_loc)

        qh = qs[:, cs]
        kh = ks[:, cs]
        vh = v_ref[:, cs]
        qb = qh.astype(BF16)
        kb = kh.astype(BF16)
        vb = vh.astype(BF16)
        c_prev = c_st[h]
        n_prev = n_st[h:h + 1, :]

        inter = b_col + m_prev
        m_t = jnp.maximum(inter, m_intra)
        s_inter = jnp.exp(inter - m_t)
        w_intra = jnp.exp(dmat - m_t) * _dot_nt(qb, kb)
        num = _dot(w_intra.astype(BF16), vb) + s_inter * _dot(qb, c_prev.astype(BF16))
        den = (jnp.sum(w_intra, axis=-1, keepdims=True)
               + s_inter * jnp.sum(qh * n_prev, axis=-1, keepdims=True))
        hh = num / jnp.maximum(jnp.abs(den), jnp.exp(-m_t))
        hn = hh * lax.rsqrt(jnp.mean(hh * hh, axis=-1, keepdims=True) + EPS)
        y_ref[:, cs] = (hn * nw_ref[:, cs] * _sigmoid(o_ref[:, cs])).astype(y_ref.dtype)

        kw = kh * w_loc
        c_loc = _dot_tn(kw.astype(BF16), vb)
        n_loc = jnp.sum(kw, axis=0, keepdims=True)
        m_new = jnp.maximum(g + m_prev, m_loc)
        s_p = jnp.exp(g + m_prev - m_new)
        s_l = jnp.exp(m_loc - m_new)
        c_st[h] = s_p * c_prev + s_l * c_loc
        n_st[h:h + 1, :] = s_p * n_prev + s_l * n_loc
        m_st[h:h + 1, :] = jnp.broadcast_to(m_new, (1, 128))


def _mlstm_call(z, zs, conv_w, bg_row, nw_row):
    S = z.shape[0]
    W = NH_M * DH_M
    L = CHUNK
    row = lambda c: (0, 0)
    return pl.pallas_call(
        _mlstm_kernel,
        grid=(S // L,),
        in_specs=[
            pl.BlockSpec((L, W), lambda c: (c, C_Q // W)),
            pl.BlockSpec((L, W), lambda c: (c, C_K // W)),
            pl.BlockSpec((L, W), lambda c: (c, C_V // W)),
            pl.BlockSpec((L, W), lambda c: (c, C_O // W)),
            pl.BlockSpec((L, 128), lambda c: (c, 0)),
            pl.BlockSpec((CONV_K, 2 * W), row),
            pl.BlockSpec((1, 128), row),
            pl.BlockSpec((1, W), row),
        ],
        out_specs=pl.BlockSpec((L, W), lambda c: (c, 0)),
        out_shape=jax.ShapeDtypeStruct((S, W), BF16),
        scratch_shapes=[
            pltpu.VMEM((8, W), F32), pltpu.VMEM((8, W), F32),
            pltpu.VMEM((L, W), F32), pltpu.VMEM((L, W), F32),
            pltpu.VMEM((NH_M, DH_M, DH_M), F32),
            pltpu.VMEM((NH_M, DH_M), F32),
            pltpu.VMEM((NH_M, 128), F32),
        ],
        compiler_params=_cparams(("arbitrary",)),
        name="mlstm",
    )(z, z, z, z, zs, conv_w, bg_row, nw_row)


def _kvprep_kernel(ka_ref, va_ref, zs_ref, kn_ref, kan_ref, vta_ref, kie_ref, kio_ref):
    tm = ka_ref.shape[0]
    for n in range(NKV_A):
        cs = slice(n * DH_A, (n + 1) * DH_A)
        k = ka_ref[:, cs]
        r = lax.rsqrt(jnp.mean(k * k, axis=-1, keepdims=True) + EPS)
        kan_ref[:, cs] = ((k * r) * kn_ref[...]).astype(BF16)
        vta_ref[n * VT_ROWS:n * VT_ROWS + DH_A, :] = va_ref[:, cs].T.astype(BF16)
        vta_ref[n * VT_ROWS + DH_A:(n + 1) * VT_ROWS, :] = jnp.ones((VT_ROWS - DH_A, tm), BF16)
    zs = zs_ref[...]
    lane = lax.broadcasted_iota(I32, zs.shape, 1)
    kie_ref[...] = jnp.where(lane < DI, zs, 0.0).astype(BF16)
    kio_ref[...] = jnp.where(lane >= DI, pltpu.roll(zs, DI, 1), 0.0).astype(BF16)


def _kvprep_call(z, zs, kn_row):
    S = z.shape[0]
    tm = 512
    W = NKV_A * DH_A
    return pl.pallas_call(
        _kvprep_kernel,
        grid=(S // tm,),
        in_specs=[
            pl.BlockSpec((tm, W), lambda m: (m, C_KA // W)),
            pl.BlockSpec((tm, W), lambda m: (m, C_VA // W)),
            pl.BlockSpec((tm, 128), lambda m: (m, 0)),
            pl.BlockSpec((1, DH_A), lambda m: (0, 0)),
        ],
        out_specs=[
            pl.BlockSpec((tm, W), lambda m: (m, 0)),
            pl.BlockSpec((NKV_A * VT_ROWS, tm), lambda m: (0, m)),
            pl.BlockSpec((tm, 128), lambda m: (m, 0)),
            pl.BlockSpec((tm, 128), lambda m: (m, 0)),
        ],
        out_shape=[
            jax.ShapeDtypeStruct((S, W), BF16),
            jax.ShapeDtypeStruct((NKV_A * VT_ROWS, S), BF16),
            jax.ShapeDtypeStruct((S, 128), BF16),
            jax.ShapeDtypeStruct((S, 128), BF16),
        ],
        compiler_params=_cparams(("arbitrary",)),
        name="kv_prep",
    )(z, z, zs, kn_row)


def _sortable(x):
    b = pltpu.bitcast(x, I32)
    return jnp.where(b < 0, b ^ jnp.int32(0x7FFFFFFF), b)


def _dsa_kernel(topk, qi_ref, qa_ref, zs_ref, kie_ref, kio_ref, kan_ref, vta_ref, qn_ref,
                tz_ref, bfar_ref, y_ref,
                skey, shi, qst, qist, thr_s, jcut_s, m_s, acc_s, lg_s):
    i = pl.program_id(0)
    S = skey.shape[0]
    R = KBLK // QBLK
    nkb = i // R + 1
    nfar = jnp.maximum((i - 1) // R, 0)
    PK = 16
    keyi = lax.broadcasted_iota(I32, (KBLK, QBLK), 0)
    qcol = lax.broadcasted_iota(I32, (KBLK, QBLK), 1) + i * QBLK

    for p in range(NH_I // 2):
        qist[p * QBLK:(p + 1) * QBLK, :] = qi_ref[:, p * 128:(p + 1) * 128].astype(BF16)
    wT = (zs_ref[...] * ((NH_I * DI) ** -0.5)).T
    for h in range(NH_A):
        q = qa_ref[:, h * DH_A:(h + 1) * DH_A]
        r = lax.rsqrt(jnp.mean(q * q, axis=-1, keepdims=True) + EPS)
        qst[h * QBLK:(h + 1) * QBLK, :] = ((q * r) * qn_ref[...] * QSCALE).astype(BF16)

    def score_block(jb, carry):
        k0 = pl.multiple_of(jb * KBLK, KBLK)
        ke = kie_ref[pl.ds(k0, KBLK), :]
        ko = kio_ref[pl.ds(k0, KBLK), :]

        def pair_scores(c):
            q = qist[c * 2 * QBLK:(c + 1) * 2 * QBLK, :]
            return _dot_nt(ke, q), _dot_nt(ko, q)

        sc = None
        nxt = pair_scores(0)
        for c in range(NH_I // 4):
            se, so = nxt
            if c + 1 < NH_I // 4:
                nxt = pair_scores(c + 1)
            for pp in range(2):
                p = 2 * c + pp
                te = jnp.maximum(se[:, pp * QBLK:(pp + 1) * QBLK], 0.0) * wT[S_WI + 2 * p:S_WI + 2 * p + 1, :]
                to = jnp.maximum(so[:, pp * QBLK:(pp + 1) * QBLK], 0.0) * wT[S_WI + 2 * p + 1:S_WI + 2 * p + 2, :]
                sc = te + to if sc is None else sc + te + to
        sc = jnp.where(sc == 0.0, 0.0, sc)
        valid = (keyi + k0) <= qcol
        skey[pl.ds(k0, KBLK), :] = jnp.where(valid, _sortable(sc), jnp.int32(INT_MIN))
        top = pltpu.bitcast(pltpu.bitcast(sc, I32) & jnp.int32(-65536), F32)
        shi[pl.ds(k0, KBLK), :] = jnp.where(valid, top, -jnp.inf).astype(BF16)
        return carry

    lax.fori_loop(0, nkb, score_block, 0)

    def count_top(trial):
        def body(jb, cnt):
            k0 = pl.multiple_of(jb * KBLK, KBLK)
            t = shi[pl.ds(k0, KBLK), :].reshape(KBLK // PK, PK, QBLK)
            hit = jnp.where(t >= trial[None], jnp.ones((), t.dtype), jnp.zeros((), t.dtype))
            n = KBLK // PK
            while n > 1:
                n //= 2
                hit = hit[:n] + hit[n:]
            return cnt + hit[0].astype(F32)
        cnt = lax.fori_loop(0, nkb, body, jnp.zeros((PK, QBLK), F32))
        return jnp.sum(cnt, axis=0, keepdims=True)

    def top_pass(b, cand):
        trial = cand ^ jnp.left_shift(jnp.int32(1), 31 - b)
        raw = jnp.where(trial < 0, trial ^ jnp.int32(0x7FFFFFFF), trial) & jnp.int32(-65536)
        tf = jnp.broadcast_to(pltpu.bitcast(raw, F32), (PK, QBLK)).astype(shi.dtype)
        cnt = count_top(tf)
        return jnp.where(cnt >= topk, trial, cand)

    cand16 = lax.fori_loop(0, 16, top_pass, jnp.full((1, QBLK), INT_MIN, I32))

    def count(pred_fn):
        def body(jb, cnt):
            k0 = pl.multiple_of(jb * KBLK, KBLK)
            hit = jnp.where(pred_fn(skey[pl.ds(k0, KBLK), :], k0), 1, 0)
            return cnt + jnp.sum(hit.reshape(KBLK // 8, 8, QBLK), axis=0)
        cnt = lax.fori_loop(0, nkb, body, jnp.zeros((8, QBLK), I32))
        return jnp.sum(cnt, axis=0, keepdims=True)

    def bit_pass(b, cand):
        trial = cand ^ jnp.left_shift(jnp.int32(1), 31 - b)
        cnt = count(lambda t, _: t >= trial)
        return jnp.where(cnt >= topk, trial, cand)

    thr = lax.fori_loop(16, 32, bit_pass, cand16)
    c_gt = count(lambda t, _: t > thr)
    c_eq = count(lambda t, _: t == thr)
    need = topk - c_gt
    short = thr == INT_MIN
    thr_s[...] = jnp.broadcast_to(jnp.where(short, jnp.int32(INT_MIN + 1), thr), (8, QBLK))
    jcut_s[...] = jnp.full((8, QBLK), S, I32)
    tie = jnp.logical_and(jnp.logical_not(short), c_eq > need)

    @pl.when(jnp.max(jnp.where(tie, 1, 0)) > 0)
    def _():
        def idx_pass(b, pos):
            t = pos + jnp.left_shift(jnp.int32(1), 13 - b)
            cnt = count(lambda tl, k0: jnp.logical_and(tl == thr, (keyi + k0) < t))
            return jnp.where(cnt < need, t, pos)
        pos = lax.fori_loop(0, 14, idx_pass, jnp.zeros((1, QBLK), I32))
        jcut_s[...] = jnp.broadcast_to(jnp.where(tie, pos, S), (8, QBLK))

    m_s[...] = jnp.full(m_s.shape, NEG, F32)
    acc_s[...] = jnp.zeros_like(acc_s)

    def logits(kstart, n):
        kn = kan_ref[pl.ds(kstart, KBLK), n * DH_A:(n + 1) * DH_A]
        return _dot_nt(kn, qst[n * GRP_A * QBLK:(n + 1) * GRP_A * QBLK, :]).astype(BF16)

    def attend(jb, last, near):
        k0 = pl.multiple_of(jb * KBLK, KBLK)
        thr_r = thr_s[0:1, :]
        jc_r = jcut_s[0:1, :]
        sk = skey[pl.ds(k0, KBLK), :]
        sel = jnp.logical_or(sk > thr_r,
                             jnp.logical_and(sk == thr_r, (keyi + k0) <= jc_r))
        negm = jnp.where(sel, 0.0, NEG).astype(BF16).reshape(KBLK // PK, PK, QBLK)
        if near:
            off = pl.multiple_of(KBLK - (i * QBLK - k0), QBLK)

        def col_max(x3):
            n = KBLK // PK
            while n > 1:
                n //= 2
                x3 = jnp.maximum(x3[:n], x3[n:])
            return jnp.max(x3[0].astype(F32), axis=0, keepdims=True)

        def packed_row(r):
            return jnp.broadcast_to(r, (PK, QBLK)).astype(BF16)[None]

        lg_next = lg_s[...]
        for n in range(NKV_A):
            lg = lg_next
            if n + 1 < NKV_A:
                lg_next = logits(k0, n + 1)
            else:
                k1 = pl.multiple_of(jnp.minimum(jb + 1, last) * KBLK, KBLK)
                lg_s[...] = logits(k1, 0)
            ps, alphas = [], []
            for g in range(GRP_A):
                h = n * GRP_A + g
                x = lg[:, g * QBLK:(g + 1) * QBLK].reshape(KBLK // PK, PK, QBLK) + negm
                m_old = m_s[h:h + 1, :]
                if near:
                    x = x + tz_ref[h, pl.ds(off, KBLK), :].reshape(KBLK // PK, PK, QBLK)
                    shift = packed_row(jnp.maximum(m_old, col_max(x)))
                    m_new = shift[0, 0:1, :].astype(F32)
                else:
                    bf = bfar_ref[h:h + 1, :]
                    shift = packed_row(jnp.maximum(m_old, col_max(x) + bf) - bf)
                    m_new = shift[0, 0:1, :].astype(F32) + bf
                p = jnp.exp2(x - shift)
                m_s[h:h + 1, :] = m_new
                alphas.append(jnp.exp2(m_old - m_new))
                ps.append(p.reshape(KBLK, QBLK))
            pt = jnp.concatenate(ps, axis=1)
            pv = _dot(vta_ref[n * VT_ROWS:(n + 1) * VT_ROWS, pl.ds(k0, KBLK)], pt)
            acc_s[n] = jnp.concatenate(alphas, axis=1) * acc_s[n] + pv

    def far_body(jb, carry):
        attend(jb, nkb - 1, False)
        return carry

    def near_body(jb, carry):
        attend(jb, nkb - 1, True)
        return carry

    lg_s[...] = logits(0, 0)
    lax.fori_loop(0, nfar, far_body, 0)
    lax.fori_loop(nfar, nkb, near_body, 0)

    for h in range(NH_A):
        n, g = divmod(h, GRP_A)
        num = acc_s[n, 0:DH_A, g * QBLK:(g + 1) * QBLK]
        den = acc_s[n, DH_A:DH_A + 1, g * QBLK:(g + 1) * QBLK]
        y_ref[:, h * DH_A:(h + 1) * DH_A] = (num / den).T.astype(y_ref.dtype)


def _dsa_call(z, zs, kie, kio, kan, vta, qn_row, tz, bfar):
    S = z.shape[0]
    topk = min(TOPK_MAX, S // 4)
    WQ = NH_A * DH_A
    WI = NH_I * DI
    res = dict(pipeline_mode=pl.Buffered(1))
    return pl.pallas_call(
        functools.partial(_dsa_kernel, topk),
        grid=(S // QBLK,),
        in_specs=[
            pl.BlockSpec((QBLK, WI), lambda i: (i, C_QI // WI)),
            pl.BlockSpec((QBLK, WQ), lambda i: (i, C_QA // WQ)),
            pl.BlockSpec((QBLK, 128), lambda i: (i, 0)),
            pl.BlockSpec((S, 128), lambda i: (0, 0), **res),
            pl.BlockSpec((S, 128), lambda i: (0, 0), **res),
            pl.BlockSpec((S, NKV_A * DH_A), lambda i: (0, 0), **res),
            pl.BlockSpec((NKV_A * VT_ROWS, S), lambda i: (0, 0), **res),
            pl.BlockSpec((1, DH_A), lambda i: (0, 0)),
            pl.BlockSpec((NH_A, 2 * KBLK, QBLK), lambda i: (0, 0, 0), **res),
            pl.BlockSpec((NH_A, QBLK), lambda i: (0, 0)),
        ],
        out_specs=pl.BlockSpec((QBLK, WQ), lambda i: (i, 0)),
        out_shape=jax.ShapeDtypeStruct((S, WQ), BF16),
        scratch_shapes=[
            pltpu.VMEM((S, QBLK), I32),
            pltpu.VMEM((S, QBLK), BF16),
            pltpu.VMEM((NH_A * QBLK, DH_A), BF16),
            pltpu.VMEM((NH_I // 2 * QBLK, 128), BF16),
            pltpu.VMEM((8, QBLK), I32),
            pltpu.VMEM((8, QBLK), I32),
            pltpu.VMEM((NH_A, QBLK), F32),
            pltpu.VMEM((NKV_A, VT_ROWS, GRP_A * QBLK), F32),
            pltpu.VMEM((KBLK, GRP_A * QBLK), BF16),
        ],
        compiler_params=_cparams(("arbitrary",)),
        name="dsa",
    )(z, z, zs, kie, kio, kan, vta, qn_row, tz, bfar)


def _t5_bucket(dist):
    n = jnp.maximum(dist, 0)
    max_exact = N_BUCKETS // 2
    nf = jnp.maximum(n, max_exact).astype(F32)
    large = max_exact + (jnp.log(nf / max_exact) / math.log(MAX_DIST / max_exact)
                         * (N_BUCKETS - max_exact)).astype(I32)
    large = jnp.minimum(large, N_BUCKETS - 1)
    return jnp.where(n < max_exact, n, large)


def _bias_tables(rel_bias):
    cc = jnp.arange(2 * KBLK, dtype=I32)[:, None]
    qq = jnp.arange(QBLK, dtype=I32)[None, :]
    bucket = _t5_bucket(qq - cc + KBLK)
    onehot = (bucket[None] == jnp.arange(N_BUCKETS, dtype=I32)[:, None, None]).astype(F32)
    rb = rel_bias.astype(F32) * LOG2E
    tz = jnp.einsum('bcq,bh->hcq', onehot, rb, precision=lax.Precision.HIGHEST)
    bfar = jnp.broadcast_to(rb[N_BUCKETS - 1][:, None], (NH_A, QBLK))
    return tz.astype(BF16), bfar


def _merge_kernel(ym_ref, ya_ref, wm_ref, wa_ref, gm_ref, ga_ref, o_ref):
    a = _dot(ym_ref[...], wm_ref[0])
    b = _dot(ya_ref[...], wa_ref[0])
    o_ref[...] = (_sigmoid(gm_ref[...]) * a + _sigmoid(ga_ref[...]) * b).astype(o_ref.dtype)


def _merge_call(l, ym, ya, wpm, wpa, z):
    S = ym.shape[0]
    tm, tn = 1024, 512
    D = D_MODEL
    return pl.pallas_call(
        _merge_kernel,
        grid=(S // tm, D // tn),
        in_specs=[
            pl.BlockSpec((tm, D), lambda m, n: (m, 0)),
            pl.BlockSpec((tm, D), lambda m, n: (m, 0)),
            pl.BlockSpec((1, D, tn), lambda m, n: (l, 0, n)),
            pl.BlockSpec((1, D, tn), lambda m, n: (l, 0, n)),
            pl.BlockSpec((tm, tn), lambda m, n: (m, C_GM // tn + n)),
            pl.BlockSpec((tm, tn), lambda m, n: (m, C_GA // tn + n)),
        ],
        out_specs=pl.BlockSpec((tm, tn), lambda m, n: (m, n)),
        out_shape=jax.ShapeDtypeStruct((S, D), BF16),
        compiler_params=_cparams(("arbitrary", "arbitrary")),
        name="merge",
    )(ym, ya, wpm, wpa, z, z)


def _out_kernel(mg_ref, wo_ref, x_ref, g1_ref, nw_ref, sc_ref, sh_ref, wr_ref, br_ref,
                xo_ref, h2_ref, lg_ref):
    xn = x_ref[...] + g1_ref[...] * _dot(mg_ref[...], wo_ref[0])
    xo_ref[...] = xn
    r = lax.rsqrt(jnp.mean(xn * xn, axis=-1, keepdims=True) + EPS)
    h = (xn * r) * nw_ref[...] * (1.0 + sc_ref[...]) + sh_ref[...]
    h2_ref[...] = _pack_halves(h)
    lg_ref[...] = _dot3(h, wr_ref[...]) + br_ref[...]


def _out_call(l, merged, wo, x2, g1, nw, sc, sh, w_router, b_router):
    S = x2.shape[0]
    tm = 256
    D = D_MODEL
    row = lambda m: (0, 0)
    return pl.pallas_call(
        _out_kernel,
        grid=(S // tm,),
        in_specs=[
            pl.BlockSpec((tm, D), lambda m: (m, 0)),
            pl.BlockSpec((1, D, D), lambda m: (l, 0, 0), pipeline_mode=pl.Buffered(1)),
            pl.BlockSpec((tm, D), lambda m: (m, 0)),
            pl.BlockSpec((1, D), row), pl.BlockSpec((1, D), row),
            pl.BlockSpec((1, D), row), pl.BlockSpec((1, D), row),
            pl.BlockSpec((D, 128), row),
            pl.BlockSpec((1, 128), row),
        ],
        out_specs=[
            pl.BlockSpec((tm, D), lambda m: (m, 0)),
            pl.BlockSpec((tm, D // 2), lambda m: (m, 0)),
            pl.BlockSpec((tm, 128), lambda m: (m, 0)),
        ],
        out_shape=[
            jax.ShapeDtypeStruct((S, D), F32),
            jax.ShapeDtypeStruct((S, D // 2), I32),
            jax.ShapeDtypeStruct((S, 128), F32),
        ],
        compiler_params=_cparams(("arbitrary",)),
        name="out_proj",
    )(merged, wo, x2, g1, nw, sc, sh, w_router, b_router)


def _pack_halves(h):
    n = h.shape[1] // 2
    bits = pltpu.bitcast(h.astype(BF16).astype(F32), I32)
    lo = lax.shift_right_logical(bits[:, :n], jnp.int32(16))
    hi = bits[:, n:] & jnp.int32(-65536)
    return hi | lo


def _unpack_halves(w):
    lo = pltpu.bitcast(lax.shift_left(w, jnp.int32(16)), F32)
    hi = pltpu.bitcast(w & jnp.int32(-65536), F32)
    return jnp.concatenate([lo, hi], axis=1).astype(BF16)


def _moe_kernel(l, be_ref, nu_ref, first_ref, slot_ref, nxt_ref, xs_ref, w1_hbm, w3_hbm, w2_hbm, o_ref,
                w1s, w3s, w2s, w1b, w3b, w2b, sem):
    b = pl.program_id(0)

    def weight_copies(e, s):
        return (pltpu.make_async_copy(w1_hbm.at[l, e], w1s.at[s], sem.at[s, 0]),
                pltpu.make_async_copy(w3_hbm.at[l, e], w3s.at[s], sem.at[s, 1]),
                pltpu.make_async_copy(w2_hbm.at[l, e], w2s.at[s], sem.at[s, 2]))

    @pl.when(jnp.logical_and(b < nu_ref[0], first_ref[b] == 1))
    def _():
        s = slot_ref[b]
        e = be_ref[b]

        @pl.when(b == 0)
        def _():
            for cp in weight_copies(e, s):
                cp.start()

        for cp in weight_copies(e, s):
            cp.wait()

        @pl.when(nxt_ref[b] >= 0)
        def _():
            for cp in weight_copies(nxt_ref[b], 1 - s):
                cp.start()

        w1b[...] = w1s[s].astype(BF16)
        w3b[...] = w3s[s].astype(BF16)
        w2b[...] = w2s[s].astype(BF16)

    @pl.when(b < nu_ref[0])
    def _():
        xb = _unpack_halves(xs_ref[...])
        a = _dot(xb, w1b[...])
        c = _dot(xb, w3b[...])
        hmid = (a * _sigmoid(a)) * c
        o_ref[...] = _dot(hmid.astype(BF16), w2b[...])

    @pl.when(b >= nu_ref[0])
    def _():
        o_ref[...] = jnp.zeros_like(o_ref)


def _combine_kernel(x_ref, y0_ref, y1_ref, gate_ref, g2_ref, *rest):
    g = gate_ref[...]
    y = y0_ref[...] * g[:, 0:1] + y1_ref[...] * g[:, 1:2]
    xo = x_ref[...] + g2_ref[...] * y
    if len(rest) == 1:
        rest[0][...] = xo
    else:
        nw_ref, sc_ref, sh_ref, ws_ref, o_ref, h_ref, zs_ref = rest
        o_ref[...] = xo
        _mixer_input(xo, nw_ref, sc_ref, sh_ref, ws_ref, h_ref, zs_ref)


def _combine_call(xn, y0, y1, gate, g2, nxt=None):
    S, D = xn.shape
    tm = 512
    blk = pl.BlockSpec((tm, D), lambda m: (m, 0))
    row = pl.BlockSpec((1, D), lambda m: (0, 0))
    in_specs = [blk, blk, blk, pl.BlockSpec((tm, TOP_K), lambda m: (m, 0)), row]
    args = [xn, y0, y1, gate, g2]
    out_specs, out_shape = blk, jax.ShapeDtypeStruct((S, D), F32)
    if nxt is not None:
        ln, nw, sc, sh, w_small = nxt
        in_specs += [row, row, row, pl.BlockSpec((1, D, 128), lambda m: (ln, 0, 0))]
        args += [nw, sc, sh, w_small]
        out_specs = [blk, blk, pl.BlockSpec((tm, 128), lambda m: (m, 0))]
        out_shape = [out_shape, jax.ShapeDtypeStruct((S, D), BF16), jax.ShapeDtypeStruct((S, 128), F32)]
    return pl.pallas_call(
        _combine_kernel,
        grid=(S // tm,),
        in_specs=in_specs,
        out_specs=out_specs,
        out_shape=out_shape,
        compiler_params=_cparams(("arbitrary",)),
        name="moe_combine",
    )(*args)


def _moe_call(l, plan, xs, w1, w3, w2):
    rows = xs.shape[0]
    nblk = rows // MOE_BLK
    D = D_MODEL
    hbm = pl.BlockSpec(memory_space=pl.ANY)
    grid_spec = pltpu.PrefetchScalarGridSpec(
        num_scalar_prefetch=len(plan),
        grid=(nblk,),
        in_specs=[pl.BlockSpec((MOE_BLK, D // 2), lambda b, *_: (b, 0)), hbm, hbm, hbm],
        out_specs=pl.BlockSpec((MOE_BLK, D), lambda b, *_: (b, 0)),
        scratch_shapes=[pltpu.VMEM((2, D, D_FF_E), F32), pltpu.VMEM((2, D, D_FF_E), F32),
                        pltpu.VMEM((2, D_FF_E, D), F32),
                        pltpu.VMEM((D, D_FF_E), BF16), pltpu.VMEM((D, D_FF_E), BF16),
                        pltpu.VMEM((D_FF_E, D), BF16),
                        pltpu.SemaphoreType.DMA((2, 3))],
    )
    return pl.pallas_call(
        functools.partial(_moe_kernel, l),
        grid_spec=grid_spec,
        out_shape=jax.ShapeDtypeStruct((rows, D), F32),
        compiler_params=_cparams(("arbitrary",)),
        name="moe_ffn",
    )(*plan, xs, w1, w3, w2)


def _route(logits):
    n = logits.shape[0]
    g_logit = logits[:, :N_GROUPS]
    g_sel = jnp.argmax(g_logit, axis=-1)
    g_oh = g_sel[:, None] == jnp.arange(N_GROUPS)[None, :]
    p_grp = jnp.sum(jnp.where(g_oh, jax.nn.softmax(g_logit, axis=-1), 0.0), axis=-1, keepdims=True)
    e_logit = logits[:, N_GROUPS:N_GROUPS + N_EXPERTS].reshape(n, N_GROUPS, EXP_PER_GROUP)
    e_logit = jnp.sum(jnp.where(g_oh[:, :, None], e_logit, 0.0), axis=1)
    top_v, top_i = lax.top_k(e_logit, TOP_K)
    gate = p_grp * jax.nn.softmax(top_v, axis=-1)
    expert = (g_sel[:, None] * EXP_PER_GROUP + top_i).reshape(-1).astype(I32)
    m = n * TOP_K
    onehot = (expert[:, None] == jnp.arange(N_EXPERTS, dtype=I32)[None, :]).astype(I32)
    rank = jnp.sum((jnp.cumsum(onehot, axis=0) - onehot) * onehot, axis=1)
    counts = jnp.sum(onehot, axis=0)
    pcounts = (counts + MOE_BLK - 1) // MOE_BLK * MOE_BLK
    pends = jnp.cumsum(pcounts)
    dest = (pends - pcounts)[expert] + rank
    nblk = -(-m // MOE_BLK) + N_EXPERTS
    rows = nblk * MOE_BLK
    tok = jnp.arange(m, dtype=I32) // TOP_K
    row_tok = (jnp.arange(rows, dtype=I32) % n).at[dest].set(tok)
    blk_start = jnp.arange(nblk, dtype=I32)[:, None] * MOE_BLK
    blk_exp = jnp.minimum(jnp.sum((pends[None, :] <= blk_start).astype(I32), axis=1),
                          N_EXPERTS - 1).astype(I32)
    nused = (pends[-1] // MOE_BLK).astype(I32).reshape(1)
    prev = jnp.concatenate([jnp.full((1,), -1, I32), blk_exp[:-1]])
    first = (blk_exp != prev).astype(I32)
    slot = (jnp.cumsum(first) - 1) & 1
    eid = jnp.arange(N_EXPERTS, dtype=I32)
    later = jnp.logical_and(eid[None, :] > eid[:, None], (counts > 0)[None, :])
    nxt_e = jnp.min(jnp.where(later, eid[None, :], N_EXPERTS), axis=1)
    nxt = jnp.where(nxt_e < N_EXPERTS, nxt_e, -1)[blk_exp].astype(I32)
    plan = (blk_exp, nused, first, slot.astype(I32), nxt)
    return row_tok, gate, plan, dest.reshape(n, TOP_K)


def _layer(l, x2, h, zs, mod, nxt, norm2, w_big, conv_w, bg_row, mlstm_norm, q_norm, k_norm,
           tz, bfar, wpm, wpa, wo, w_router, b_router, w1, w3, w2):
    D = D_MODEL
    sh1, sc1, g1, sh2, sc2, g2 = [mod[:, j * D:(j + 1) * D] for j in range(6)]
    z = _in_call(l, h, w_big)
    ym = _mlstm_call(z, zs, conv_w, bg_row, mlstm_norm[None, :])
    kan, vta, kie, kio = _kvprep_call(z, zs, k_norm[None, :])
    ya = _dsa_call(z, zs, kie, kio, kan, vta, q_norm[None, :], tz, bfar)
    merged = _merge_call(l, ym, ya, wpm, wpa, z)
    xn, h2, logits = _out_call(l, merged, wo, x2, g1, norm2[None, :], sc2, sh2, w_router, b_router)
    row_tok, gate, plan, dest = _route(logits)
    xs = h2[row_tok]
    yb = _moe_call(l, plan, xs, w1, w3, w2)
    return _combine_call(xn, yb[dest[:, 0]], yb[dest[:, 1]], gate, g2, nxt)


def kernel(x, c, w_ada, b_ada, norm1, norm2, w_in, conv_w, b_gate, mlstm_norm, q_norm, k_norm,
           rel_bias, w_proj_m, w_proj_a, w_out, w_grp, b_grp, w_exp, b_exp, w1, w3, w2):
    B, S, D = x.shape
    assert B == 1 and D == D_MODEL
    x2 = x.reshape(S, D)
    c16 = jnp.pad(c, ((0, 16 - B), (0, 0)))
    mods = _ada_call(c16, w_ada, b_ada[:, None, :])
    tz, bfar = _bias_tables(rel_bias)
    w_big, w_small = _wprep_call(jnp.swapaxes(w_in, 1, 2))
    wpm = w_proj_m.astype(BF16)
    wpa = w_proj_a.astype(BF16)
    wo = w_out.astype(BF16)

    def norm1_params(l):
        return (l, norm1[l][None, :], mods[l, 0:1, D:2 * D], mods[l, 0:1, 0:D], w_small)

    first = norm1_params(0)
    h, zs = _prenorm_call(first[0], x2, *first[1:])
    for l in range(DEPTH):
        bg_row = jnp.zeros((1, 128), F32).at[0, S_IP:S_IP + 2 * NH_M].set(b_gate[l])
        w_router = jnp.concatenate([w_grp[l], w_exp[l],
                                    jnp.zeros((D, 128 - N_GROUPS - N_EXPERTS), F32)], axis=1)
        b_router = jnp.concatenate([b_grp[l], b_exp[l],
                                    jnp.zeros((128 - N_GROUPS - N_EXPERTS,), F32)])[None, :]
        nxt = norm1_params(l + 1) if l + 1 < DEPTH else None
        out = _layer(l, x2, h, zs, mods[l, 0:1, :], nxt, norm2[l], w_big, conv_w[l], bg_row,
                     mlstm_norm[l], q_norm[l], k_norm[l], tz, bfar, wpm, wpa, wo,
                     w_router, b_router, w1, w3, w2)
        if nxt is None:
            x2 = out
        else:
            x2, h, zs = out
    return x2.reshape(B, S, D)
```

```python
import functools
import math

import numpy as np
import jax
import jax.numpy as jnp
from jax import lax
from jax.experimental import pallas as pl
from jax.experimental.pallas import tpu as pltpu

F32 = jnp.float32
BF16 = jnp.bfloat16
I32 = jnp.int32

D_MODEL = 2048
DEPTH = 4
NH_M = 8
DH_M = 256
CONV_K = 4
CHUNK = 128
DH_A = 128
NH_A = 16
NKV_A = 4
GRP_A = NH_A // NKV_A
NH_I = 16
DI = 64
TOPK_MAX = 256
QBLK = 128
KBLK = 512
N_BUCKETS = 32
MAX_DIST = 128
N_GROUPS = 4
EXP_PER_GROUP = 8
N_EXPERTS = 32
TOP_K = 2
D_FF_E = 512
MOE_BLK = 128
EPS = 1e-6

IN_SIZES = (2 * NH_M * DH_M, NH_M * DH_M, NH_M * DH_M, NH_M, NH_M, NH_A * DH_A, NKV_A * DH_A,
            NKV_A * DH_A, NH_I * DI, DI, NH_I, D_MODEL, D_MODEL)
D_IN = sum(IN_SIZES)
C_Q, C_K, C_V, C_O = 0, 2048, 4096, 6144
C_QA, C_KA, C_VA, C_QI, C_GM, C_GA = 8192, 10240, 10752, 11264, 12288, 14336
N_BIG = 16384
S_KI, S_WI, S_IP, S_FP = 0, 64, 80, 88

INT_MIN = -(2 ** 31)
NEG = -(2.0 ** 100)
LOG2E = math.log2(math.e)
QSCALE = DH_A ** -0.5 * LOG2E
VT_ROWS = DH_A + 16
VMEM_LIMIT = 56 * 1024 * 1024


def _cparams(sem):
    return pltpu.CompilerParams(dimension_semantics=sem, vmem_limit_bytes=VMEM_LIMIT)


def _split(a):
    hi = a.astype(BF16)
    lo = (a - hi.astype(F32)).astype(BF16)
    return hi, lo


def _dot(a, b):
    return jnp.dot(a, b, preferred_element_type=F32)


def _dot3(a, b):
    ah, al = _split(a)
    bh, bl = _split(b)
    return _dot(ah, bh) + (_dot(al, bh) + _dot(ah, bl))


def _dot_nt(a, b):
    return lax.dot_general(a, b, (((1,), (1,)), ((), ())), preferred_element_type=F32)


def _dot_tn(a, b):
    return lax.dot_general(a, b, (((0,), (0,)), ((), ())), preferred_element_type=F32)


def _sigmoid(x):
    return 1.0 / (1.0 + jnp.exp(-x))


def _ada_kernel(c_ref, w_ref, b_ref, o_ref):
    c = c_ref[...]
    ca = c * _sigmoid(c)
    o_ref[0] = _dot3(ca, w_ref[0]) + b_ref[0]


def _ada_call(c16, w_ada, b_ada3):
    tn = 1024
    n6 = w_ada.shape[-1]
    return pl.pallas_call(
        _ada_kernel,
        grid=(DEPTH, n6 // tn),
        in_specs=[
            pl.BlockSpec((16, D_MODEL), lambda l, n: (0, 0)),
            pl.BlockSpec((1, D_MODEL, tn), lambda l, n: (l, 0, n)),
            pl.BlockSpec((1, 1, tn), lambda l, n: (l, 0, n)),
        ],
        out_specs=pl.BlockSpec((1, 16, tn), lambda l, n: (l, 0, n)),
        out_shape=jax.ShapeDtypeStruct((DEPTH, 16, n6), F32),
        compiler_params=_cparams(("arbitrary", "arbitrary")),
        name="adaln",
    )(c16, w_ada, b_ada3)


def _wprep_kernel(w_ref, wb_ref, ws_ref):
    x = w_ref[0]
    cuts = [0] + [int(t) for t in np.cumsum(IN_SIZES)]
    seg = lambda a, b: x[cuts[a]:cuts[b], :]
    wb_ref[0] = jnp.concatenate([seg(0, 3), seg(5, 9), seg(11, 13)], axis=0).T.astype(BF16)
    ws_ref[0] = jnp.concatenate([seg(9, 11), seg(3, 5), jnp.zeros((32, x.shape[1]), F32)], axis=0).T


def _wprep_call(w_in_t):
    tm = 128
    return pl.pallas_call(
        _wprep_kernel,
        grid=(DEPTH, D_MODEL // tm),
        in_specs=[pl.BlockSpec((1, D_IN, tm), lambda l, m: (l, 0, m))],
        out_specs=[
            pl.BlockSpec((1, tm, N_BIG), lambda l, m: (l, m, 0)),
            pl.BlockSpec((1, tm, 128), lambda l, m: (l, m, 0)),
        ],
        out_shape=[
            jax.ShapeDtypeStruct((DEPTH, D_MODEL, N_BIG), BF16),
            jax.ShapeDtypeStruct((DEPTH, D_MODEL, 128), F32),
        ],
        compiler_params=_cparams(("arbitrary", "arbitrary")),
        name="w_prep",
    )(w_in_t)


def _mixer_input(x, nw_ref, sc_ref, sh_ref, ws_ref, h_ref, zs_ref):
    r = lax.rsqrt(jnp.mean(x * x, axis=-1, keepdims=True) + EPS)
    h = (x * r) * nw_ref[...] * (1.0 + sc_ref[...]) + sh_ref[...]
    h_ref[...] = h.astype(BF16)
    zs_ref[...] = _dot3(h, ws_ref[0])


def _prenorm_kernel(x_ref, nw_ref, sc_ref, sh_ref, ws_ref, h_ref, zs_ref):
    _mixer_input(x_ref[...], nw_ref, sc_ref, sh_ref, ws_ref, h_ref, zs_ref)


def _prenorm_call(l, x2, nw, sc, sh, w_small):
    S, D = x2.shape
    tm = 512
    row = lambda m: (0, 0)
    return pl.pallas_call(
        _prenorm_kernel,
        grid=(S // tm,),
        in_specs=[
            pl.BlockSpec((tm, D), lambda m: (m, 0)),
            pl.BlockSpec((1, D), row), pl.BlockSpec((1, D), row), pl.BlockSpec((1, D), row),
            pl.BlockSpec((1, D, 128), lambda m: (l, 0, 0)),
        ],
        out_specs=[pl.BlockSpec((tm, D), lambda m: (m, 0)), pl.BlockSpec((tm, 128), lambda m: (m, 0))],
        out_shape=[jax.ShapeDtypeStruct((S, D), BF16), jax.ShapeDtypeStruct((S, 128), F32)],
        compiler_params=_cparams(("arbitrary",)),
        name="prenorm",
    )(x2, nw, sc, sh, w_small)


def _in_kernel(h_ref, wb_ref, z_ref):
    z_ref[...] = _dot(h_ref[...], wb_ref[0])


def _in_call(l, h, w_big):
    S = h.shape[0]
    tm, tn = 1024, 1024
    return pl.pallas_call(
        _in_kernel,
        grid=(S // tm, N_BIG // tn),
        in_specs=[
            pl.BlockSpec((tm, D_MODEL), lambda m, n: (m, 0)),
            pl.BlockSpec((1, D_MODEL, tn), lambda m, n: (l, 0, n)),
        ],
        out_specs=pl.BlockSpec((tm, tn), lambda m, n: (m, n)),
        out_shape=jax.ShapeDtypeStruct((S, N_BIG), F32),
        compiler_params=_cparams(("arbitrary", "arbitrary")),
        name="in_proj",
    )(h, w_big)


def _mlstm_kernel(q_ref, k_ref, v_ref, o_ref, zs_ref, cw_ref, bg_ref, nw_ref, y_ref,
                  qprev, kprev, qs, ks, c_st, n_st, m_st):
    L = CHUNK

    @pl.when(pl.program_id(0) == 0)
    def _():
        qprev[...] = jnp.zeros_like(qprev)
        kprev[...] = jnp.zeros_like(kprev)
        c_st[...] = jnp.zeros_like(c_st)
        n_st[...] = jnp.zeros_like(n_st)
        m_st[...] = jnp.zeros_like(m_st)

    row8 = lax.broadcasted_iota(I32, (8, NH_M * DH_M), 0)

    def conv_silu(cur_ref, tail_ref, woff):
        cur = cur_ref[...]
        tail = tail_ref[...]
        out = None
        for j in range(CONV_K):
            s = CONV_K - 1 - j
            if s == 0:
                u = cur
            else:
                r = pltpu.roll(cur, s, 0)
                head = jnp.where(row8 < s, pltpu.roll(tail, s, 0), r[0:8])
                u = jnp.concatenate([head, r[8:]], axis=0)
            term = u * cw_ref[j:j + 1, woff:woff + NH_M * DH_M]
            out = term if out is None else out + term
        tail_ref[...] = cur[L - 8:L]
        return out * _sigmoid(out)

    qs[...] = conv_silu(q_ref, qprev, 0)
    ks[...] = conv_silu(k_ref, kprev, NH_M * DH_M) * (DH_M ** -0.5)

    G = zs_ref[...] + bg_ref[...]
    LF = jnp.minimum(G, 0.0) - jnp.log1p(jnp.exp(-jnp.abs(G)))
    ri = lax.broadcasted_iota(I32, (L, L), 0)
    ci = lax.broadcasted_iota(I32, (L, L), 1)
    causal = ci <= ri
    tril = jnp.where(causal, 1.0, 0.0).astype(BF16)
    lf_hi, lf_lo = _split(LF)
    lf_lo2 = (LF - lf_hi.astype(F32) - lf_lo.astype(F32)).astype(BF16)
    B = _dot(tril, lf_hi) + (_dot(tril, lf_lo) + _dot(tril, lf_lo2))
    GT = G.T
    BT = B.T

    for h in range(NH_M):
        cs = slice(h * DH_M, (h + 1) * DH_M)
        b_col = B[:, S_FP + h:S_FP + h + 1]
        ig_col = G[:, S_IP + h:S_IP + h + 1]
        b_row = BT[S_FP + h:S_FP + h + 1, :]
        ig_row = GT[S_IP + h:S_IP + h + 1, :]
        g = B[L - 1:L, S_FP + h:S_FP + h + 1]
        m_prev = m_st[h:h + 1, 0:1]

        dmat = jnp.where(causal, b_col - b_row + ig_row, -jnp.inf)
        m_intra = jnp.max(dmat, axis=-1, keepdims=True)
        a_col = g - b_col + ig_col
        m_loc = jnp.max(a_col, axis=0, keepdims=True)
        w_loc = jnp.exp(a_col - m_loc)

        qh = qs[:, cs]
        kh = ks[:, cs]
        vh = v_ref[:, cs]
        qb = qh.astype(BF16)
        kb = kh.astype(BF16)
        vb = vh.astype(BF16)
        c_prev = c_st[h]
        n_prev = n_st[h:h + 1, :]

        inter = b_col + m_prev
        m_t = jnp.maximum(inter, m_intra)
        s_inter = jnp.exp(inter - m_t)
        w_intra = jnp.exp(dmat - m_t) * _dot_nt(qb, kb)
        num = _dot(w_intra.astype(BF16), vb) + s_inter * _dot(qb, c_prev.astype(BF16))
        den = (jnp.sum(w_intra, axis=-1, keepdims=True)
               + s_inter * jnp.sum(qh * n_prev, axis=-1, keepdims=True))
        hh = num / jnp.maximum(jnp.abs(den), jnp.exp(-m_t))
        hn = hh * lax.rsqrt(jnp.mean(hh * hh, axis=-1, keepdims=True) + EPS)
        y_ref[:, cs] = (hn * nw_ref[:, cs] * _sigmoid(o_ref[:, cs])).astype(y_ref.dtype)

        kw = kh * w_loc
        c_loc = _dot_tn(kw.astype(BF16), vb)
        n_loc = jnp.sum(kw, axis=0, keepdims=True)
        m_new = jnp.maximum(g + m_prev, m_loc)
        s_p = jnp.exp(g + m_prev - m_new)
        s_l = jnp.exp(m_loc - m_new)
        c_st[h] = s_p * c_prev + s_l * c_loc
        n_st[h:h + 1, :] = s_p * n_prev + s_l * n_loc
        m_st[h:h + 1, :] = jnp.broadcast_to(m_new, (1, 128))


def _mlstm_call(z, zs, conv_w, bg_row, nw_row):
    S = z.shape[0]
    W = NH_M * DH_M
    L = CHUNK
    row = lambda c: (0, 0)
    return pl.pallas_call(
        _mlstm_kernel,
        grid=(S // L,),
        in_specs=[
            pl.BlockSpec((L, W), lambda c: (c, C_Q // W)),
            pl.BlockSpec((L, W), lambda c: (c, C_K // W)),
            pl.BlockSpec((L, W), lambda c: (c, C_V // W)),
            pl.BlockSpec((L, W), lambda c: (c, C_O // W)),
            pl.BlockSpec((L, 128), lambda c: (c, 0)),
            pl.BlockSpec((CONV_K, 2 * W), row),
            pl.BlockSpec((1, 128), row),
            pl.BlockSpec((1, W), row),
        ],
        out_specs=pl.BlockSpec((L, W), lambda c: (c, 0)),
        out_shape=jax.ShapeDtypeStruct((S, W), BF16),
        scratch_shapes=[
            pltpu.VMEM((8, W), F32), pltpu.VMEM((8, W), F32),
            pltpu.VMEM((L, W), F32), pltpu.VMEM((L, W), F32),
            pltpu.VMEM((NH_M, DH_M, DH_M), F32),
            pltpu.VMEM((NH_M, DH_M), F32),
            pltpu.VMEM((NH_M, 128), F32),
        ],
        compiler_params=_cparams(("arbitrary",)),
        name="mlstm",
    )(z, z, z, z, zs, conv_w, bg_row, nw_row)


def _kvprep_kernel(ka_ref, va_ref, zs_ref, kn_ref, kan_ref, vta_ref, kie_ref, kio_ref):
    tm = ka_ref.shape[0]
    for n in range(NKV_A):
        cs = slice(n * DH_A, (n + 1) * DH_A)
        k = ka_ref[:, cs]
        r = lax.rsqrt(jnp.mean(k * k, axis=-1, keepdims=True) + EPS)
        kan_ref[:, cs] = ((k * r) * kn_ref[...]).astype(BF16)
        vta_ref[n * VT_ROWS:n * VT_ROWS + DH_A, :] = va_ref[:, cs].T.astype(BF16)
        vta_ref[n * VT_ROWS + DH_A:(n + 1) * VT_ROWS, :] = jnp.ones((VT_ROWS - DH_A, tm), BF16)
    zs = zs_ref[...]
    lane = lax.broadcasted_iota(I32, zs.shape, 1)
    kie_ref[...] = jnp.where(lane < DI, zs, 0.0).astype(BF16)
    kio_ref[...] = jnp.where(lane >= DI, pltpu.roll(zs, DI, 1), 0.0).astype(BF16)


def _kvprep_call(z, zs, kn_row):
    S = z.shape[0]
    tm = 512
    W = NKV_A * DH_A
    return pl.pallas_call(
        _kvprep_kernel,
        grid=(S // tm,),
        in_specs=[
            pl.BlockSpec((tm, W), lambda m: (m, C_KA // W)),
            pl.BlockSpec((tm, W), lambda m: (m, C_VA // W)),
            pl.BlockSpec((tm, 128), lambda m: (m, 0)),
            pl.BlockSpec((1, DH_A), lambda m: (0, 0)),
        ],
        out_specs=[
            pl.BlockSpec((tm, W), lambda m: (m, 0)),
            pl.BlockSpec((NKV_A * VT_ROWS, tm), lambda m: (0, m)),
            pl.BlockSpec((tm, 128), lambda m: (m, 0)),
            pl.BlockSpec((tm, 128), lambda m: (m, 0)),
        ],
        out_shape=[
            jax.ShapeDtypeStruct((S, W), BF16),
            jax.ShapeDtypeStruct((NKV_A * VT_ROWS, S), BF16),
            jax.ShapeDtypeStruct((S, 128), BF16),
            jax.ShapeDtypeStruct((S, 128), BF16),
        ],
        compiler_params=_cparams(("arbitrary",)),
        name="kv_prep",
    )(z, z, zs, kn_row)


def _sortable(x):
    b = pltpu.bitcast(x, I32)
    return jnp.where(b < 0, b ^ jnp.int32(0x7FFFFFFF), b)


def _dsa_kernel(topk, qi_ref, qa_ref, zs_ref, kie_ref, kio_ref, kan_ref, vta_ref, qn_ref,
                tz_ref, bfar_ref, y_ref,
                skey, shi, qst, qist, thr_s, jcut_s, m_s, acc_s, lg_s):
    i = pl.program_id(0)
    S = skey.shape[0]
    R = KBLK // QBLK
    nkb = i // R + 1
    nfar = jnp.maximum((i - 1) // R, 0)
    keyi = lax.broadcasted_iota(I32, (KBLK, QBLK), 0)
    qcol = lax.broadcasted_iota(I32, (KBLK, QBLK), 1) + i * QBLK

    for p in range(NH_I // 2):
        qist[p * QBLK:(p + 1) * QBLK, :] = qi_ref[:, p * 128:(p + 1) * 128].astype(BF16)
    wT = (zs_ref[...] * ((NH_I * DI) ** -0.5)).T
    for h in range(NH_A):
        q = qa_ref[:, h * DH_A:(h + 1) * DH_A]
        r = lax.rsqrt(jnp.mean(q * q, axis=-1, keepdims=True) + EPS)
        qst[h * QBLK:(h + 1) * QBLK, :] = ((q * r) * qn_ref[...] * QSCALE).astype(BF16)

    def score_block(jb, carry):
        k0 = pl.multiple_of(jb * KBLK, KBLK)
        ke = kie_ref[pl.ds(k0, KBLK), :]
        ko = kio_ref[pl.ds(k0, KBLK), :]

        def pair_scores(c):
            q = qist[c * 2 * QBLK:(c + 1) * 2 * QBLK, :]
            return _dot_nt(ke, q), _dot_nt(ko, q)

        sc = None
        nxt = pair_scores(0)
        for c in range(NH_I // 4):
            se, so = nxt
            if c + 1 < NH_I // 4:
                nxt = pair_scores(c + 1)
            for pp in range(2):
                p = 2 * c + pp
                te = jnp.maximum(se[:, pp * QBLK:(pp + 1) * QBLK], 0.0) * wT[S_WI + 2 * p:S_WI + 2 * p + 1, :]
                to = jnp.maximum(so[:, pp * QBLK:(pp + 1) * QBLK], 0.0) * wT[S_WI + 2 * p + 1:S_WI + 2 * p + 2, :]
                sc = te + to if sc is None else sc + te + to
        sc = jnp.where(sc == 0.0, 0.0, sc)
        valid = (keyi + k0) <= qcol
        skey[pl.ds(k0, KBLK), :] = jnp.where(valid, _sortable(sc), jnp.int32(INT_MIN))
        top = pltpu.bitcast(pltpu.bitcast(sc, I32) & jnp.int32(-65536), F32)
        shi[pl.ds(k0, KBLK), :] = jnp.where(valid, top, -jnp.inf).astype(BF16)
        return carry

    lax.fori_loop(0, nkb, score_block, 0)

    PK = 16

    def count_top(trial):
        def body(jb, cnt):
            k0 = pl.multiple_of(jb * KBLK, KBLK)
            t = shi[pl.ds(k0, KBLK), :].reshape(KBLK // PK, PK, QBLK)
            hit = jnp.where(t >= trial[None], jnp.ones((), t.dtype), jnp.zeros((), t.dtype))
            n = KBLK // PK
            while n > 1:
                n //= 2
                hit = hit[:n] + hit[n:]
            return cnt + hit[0].astype(F32)
        cnt = lax.fori_loop(0, nkb, body, jnp.zeros((PK, QBLK), F32))
        return jnp.sum(cnt, axis=0, keepdims=True)

    def top_pass(b, carry):
        cand, c_ge = carry
        trial = cand ^ jnp.left_shift(jnp.int32(1), 31 - b)
        raw = jnp.where(trial < 0, trial ^ jnp.int32(0x7FFFFFFF), trial) & jnp.int32(-65536)
        tiny = jnp.logical_and((raw & jnp.int32(0x7F800000)) == 0, raw > 0)
        raw = jnp.where(tiny, jnp.int32(0x00800000), raw)
        tf = jnp.broadcast_to(pltpu.bitcast(raw, F32), (PK, QBLK)).astype(shi.dtype)
        cnt = count_top(tf).astype(I32)
        ok = cnt >= topk
        return jnp.where(ok, trial, cand), jnp.where(ok, cnt, c_ge)

    start = (jnp.full((1, QBLK), INT_MIN, I32), jnp.zeros((1, QBLK), I32))
    cand16 = lax.fori_loop(0, 16, top_pass, start)

    def count(pred_fn):
        def body(jb, cnt):
            k0 = pl.multiple_of(jb * KBLK, KBLK)
            hit = jnp.where(pred_fn(skey[pl.ds(k0, KBLK), :], k0), 1, 0)
            return cnt + jnp.sum(hit.reshape(KBLK // 8, 8, QBLK), axis=0)
        cnt = lax.fori_loop(0, nkb, body, jnp.zeros((8, QBLK), I32))
        return jnp.sum(cnt, axis=0, keepdims=True)

    def bit_pass(b, carry):
        cand, c_ge = carry
        trial = cand ^ jnp.left_shift(jnp.int32(1), 31 - b)
        cnt = count(lambda t, _: t >= trial)
        ok = cnt >= topk
        return jnp.where(ok, trial, cand), jnp.where(ok, cnt, c_ge)

    thr, c_ge = lax.fori_loop(16, 32, bit_pass, cand16)
    short = thr == INT_MIN
    thr_s[...] = jnp.broadcast_to(jnp.where(short, jnp.int32(INT_MIN + 1), thr), (8, QBLK))
    jcut_s[...] = jnp.full((8, QBLK), S, I32)
    tie = jnp.logical_and(jnp.logical_not(short), c_ge > topk)

    @pl.when(jnp.max(jnp.where(tie, 1, 0)) > 0)
    def _():
        need = topk - count(lambda t, _: t > thr)

        def idx_pass(b, pos):
            t = pos + jnp.left_shift(jnp.int32(1), 13 - b)
            cnt = count(lambda tl, k0: jnp.logical_and(tl == thr, (keyi + k0) < t))
            return jnp.where(cnt < need, t, pos)
        pos = lax.fori_loop(0, 14, idx_pass, jnp.zeros((1, QBLK), I32))
        jcut_s[...] = jnp.broadcast_to(jnp.where(tie, pos, S), (8, QBLK))

    m_s[...] = jnp.full(m_s.shape, NEG, F32)
    acc_s[...] = jnp.zeros_like(acc_s)

    def logits(kstart, n):
        kn = kan_ref[pl.ds(kstart, KBLK), n * DH_A:(n + 1) * DH_A]
        return _dot_nt(kn, qst[n * GRP_A * QBLK:(n + 1) * GRP_A * QBLK, :])

    def attend(jb, last, near):
        k0 = pl.multiple_of(jb * KBLK, KBLK)
        thr_r = thr_s[0:1, :]
        jc_r = jcut_s[0:1, :]
        sk = skey[pl.ds(k0, KBLK), :]
        sel = jnp.logical_or(sk > thr_r,
                             jnp.logical_and(sk == thr_r, (keyi + k0) <= jc_r))
        negm = jnp.where(sel, 0.0, NEG).astype(BF16).reshape(KBLK // PK, PK, QBLK)
        if near:
            off = pl.multiple_of(KBLK - (i * QBLK - k0), QBLK)

        def col_max(x3):
            n = KBLK // PK
            while n > 1:
                n //= 2
                x3 = jnp.maximum(x3[:n], x3[n:])
            return jnp.max(x3[0].astype(F32), axis=0, keepdims=True)

        def packed_row(r):
            return jnp.broadcast_to(r, (PK, QBLK)).astype(BF16)[None]

        lg_next = lg_s[...]
        for n in range(NKV_A):
            lg = lg_next
            if n + 1 < NKV_A:
                lg_next = logits(k0, n + 1)
            else:
                k1 = pl.multiple_of(jnp.minimum(jb + 1, last) * KBLK, KBLK)
                lg_s[...] = logits(k1, 0)
            ps, alphas = [], []
            for g in range(GRP_A):
                h = n * GRP_A + g
                x = lg[:, g * QBLK:(g + 1) * QBLK].astype(BF16).reshape(KBLK // PK, PK, QBLK) + negm
                m_old = m_s[h:h + 1, :]
                if near:
                    x = x + tz_ref[h, pl.ds(off, KBLK), :].reshape(KBLK // PK, PK, QBLK)
                    shift = packed_row(jnp.maximum(m_old, col_max(x)))
                    m_new = shift[0, 0:1, :].astype(F32)
                else:
                    bf = bfar_ref[h:h + 1, :]
                    shift = packed_row(jnp.maximum(m_old, col_max(x) + bf) - bf)
                    m_new = shift[0, 0:1, :].astype(F32) + bf
                p = jnp.exp2(x - shift)
                m_s[h:h + 1, :] = m_new
                alphas.append(jnp.exp2(m_old - m_new))
                ps.append(p.reshape(KBLK, QBLK))
            pt = jnp.concatenate(ps, axis=1)
            pv = _dot(vta_ref[n * VT_ROWS:(n + 1) * VT_ROWS, pl.ds(k0, KBLK)], pt)
            acc_s[n] = jnp.concatenate(alphas, axis=1) * acc_s[n] + pv

    def far_body(jb, carry):
        attend(jb, nkb - 1, False)
        return carry

    def near_body(jb, carry):
        attend(jb, nkb - 1, True)
        return carry

    lg_s[...] = logits(0, 0)
    lax.fori_loop(0, nfar, far_body, 0)
    lax.fori_loop(nfar, nkb, near_body, 0)

    for h in range(NH_A):
        n, g = divmod(h, GRP_A)
        num = acc_s[n, 0:DH_A, g * QBLK:(g + 1) * QBLK]
        den = acc_s[n, DH_A:DH_A + 1, g * QBLK:(g + 1) * QBLK]
        y_ref[:, h * DH_A:(h + 1) * DH_A] = (num / den).T.astype(y_ref.dtype)


def _dsa_call(z, zs, kie, kio, kan, vta, qn_row, tz, bfar):
    S = z.shape[0]
    topk = min(TOPK_MAX, S // 4)
    WQ = NH_A * DH_A
    WI = NH_I * DI
    res = dict(pipeline_mode=pl.Buffered(1))
    return pl.pallas_call(
        functools.partial(_dsa_kernel, topk),
        grid=(S // QBLK,),
        in_specs=[
            pl.BlockSpec((QBLK, WI), lambda i: (i, C_QI // WI)),
            pl.BlockSpec((QBLK, WQ), lambda i: (i, C_QA // WQ)),
            pl.BlockSpec((QBLK, 128), lambda i: (i, 0)),
            pl.BlockSpec((S, 128), lambda i: (0, 0), **res),
            pl.BlockSpec((S, 128), lambda i: (0, 0), **res),
            pl.BlockSpec((S, NKV_A * DH_A), lambda i: (0, 0), **res),
            pl.BlockSpec((NKV_A * VT_ROWS, S), lambda i: (0, 0), **res),
            pl.BlockSpec((1, DH_A), lambda i: (0, 0)),
            pl.BlockSpec((NH_A, 2 * KBLK, QBLK), lambda i: (0, 0, 0), **res),
            pl.BlockSpec((NH_A, QBLK), lambda i: (0, 0)),
        ],
        out_specs=pl.BlockSpec((QBLK, WQ), lambda i: (i, 0)),
        out_shape=jax.ShapeDtypeStruct((S, WQ), BF16),
        scratch_shapes=[
            pltpu.VMEM((S, QBLK), I32),
            pltpu.VMEM((S, QBLK), BF16),
            pltpu.VMEM((NH_A * QBLK, DH_A), BF16),
            pltpu.VMEM((NH_I // 2 * QBLK, 128), BF16),
            pltpu.VMEM((8, QBLK), I32),
            pltpu.VMEM((8, QBLK), I32),
            pltpu.VMEM((NH_A, QBLK), F32),
            pltpu.VMEM((NKV_A, VT_ROWS, GRP_A * QBLK), F32),
            pltpu.VMEM((KBLK, GRP_A * QBLK), F32),
        ],
        compiler_params=_cparams(("arbitrary",)),
        name="dsa",
    )(z, z, zs, kie, kio, kan, vta, qn_row, tz, bfar)


def _t5_bucket(dist):
    n = jnp.maximum(dist, 0)
    max_exact = N_BUCKETS // 2
    nf = jnp.maximum(n, max_exact).astype(F32)
    large = max_exact + (jnp.log(nf / max_exact) / math.log(MAX_DIST / max_exact)
                         * (N_BUCKETS - max_exact)).astype(I32)
    large = jnp.minimum(large, N_BUCKETS - 1)
    return jnp.where(n < max_exact, n, large)


def _bias_tables(rel_bias):
    cc = jnp.arange(2 * KBLK, dtype=I32)[:, None]
    qq = jnp.arange(QBLK, dtype=I32)[None, :]
    bucket = _t5_bucket(qq - cc + KBLK)
    onehot = (bucket[None] == jnp.arange(N_BUCKETS, dtype=I32)[:, None, None]).astype(F32)
    rb = rel_bias.astype(F32) * LOG2E
    tz = jnp.einsum('bcq,bh->hcq', onehot, rb, precision=lax.Precision.HIGHEST)
    bfar = jnp.broadcast_to(rb[N_BUCKETS - 1][:, None], (NH_A, QBLK))
    return tz.astype(BF16), bfar


def _merge_kernel(ym_ref, ya_ref, wm_ref, wa_ref, gm_ref, ga_ref, o_ref):
    a = _dot(ym_ref[...], wm_ref[0])
    b = _dot(ya_ref[...], wa_ref[0])
    o_ref[...] = (_sigmoid(gm_ref[...]) * a + _sigmoid(ga_ref[...]) * b).astype(o_ref.dtype)


def _merge_call(l, ym, ya, wpm, wpa, z):
    S = ym.shape[0]
    tm, tn = 1024, 512
    D = D_MODEL
    return pl.pallas_call(
        _merge_kernel,
        grid=(S // tm, D // tn),
        in_specs=[
            pl.BlockSpec((tm, D), lambda m, n: (m, 0)),
            pl.BlockSpec((tm, D), lambda m, n: (m, 0)),
            pl.BlockSpec((1, D, tn), lambda m, n: (l, 0, n)),
            pl.BlockSpec((1, D, tn), lambda m, n: (l, 0, n)),
            pl.BlockSpec((tm, tn), lambda m, n: (m, C_GM // tn + n)),
            pl.BlockSpec((tm, tn), lambda m, n: (m, C_GA // tn + n)),
        ],
        out_specs=pl.BlockSpec((tm, tn), lambda m, n: (m, n)),
        out_shape=jax.ShapeDtypeStruct((S, D), BF16),
        compiler_params=_cparams(("arbitrary", "arbitrary")),
        name="merge",
    )(ym, ya, wpm, wpa, z, z)


def _out_kernel(mg_ref, wo_ref, x_ref, g1_ref, nw_ref, sc_ref, sh_ref, wr_ref, br_ref,
                xo_ref, h2_ref, lg_ref):
    xn = x_ref[...] + g1_ref[...] * _dot(mg_ref[...], wo_ref[0])
    xo_ref[...] = xn
    r = lax.rsqrt(jnp.mean(xn * xn, axis=-1, keepdims=True) + EPS)
    h = (xn * r) * nw_ref[...] * (1.0 + sc_ref[...]) + sh_ref[...]
    h2_ref[...] = _pack_halves(h)
    lg_ref[...] = _dot3(h, wr_ref[...]) + br_ref[...]


def _out_call(l, merged, wo, x2, g1, nw, sc, sh, w_router, b_router):
    S = x2.shape[0]
    tm = 256
    D = D_MODEL
    row = lambda m: (0, 0)
    return pl.pallas_call(
        _out_kernel,
        grid=(S // tm,),
        in_specs=[
            pl.BlockSpec((tm, D), lambda m: (m, 0)),
            pl.BlockSpec((1, D, D), lambda m: (l, 0, 0), pipeline_mode=pl.Buffered(1)),
            pl.BlockSpec((tm, D), lambda m: (m, 0)),
            pl.BlockSpec((1, D), row), pl.BlockSpec((1, D), row),
            pl.BlockSpec((1, D), row), pl.BlockSpec((1, D), row),
            pl.BlockSpec((D, 128), row),
            pl.BlockSpec((1, 128), row),
        ],
        out_specs=[
            pl.BlockSpec((tm, D), lambda m: (m, 0)),
            pl.BlockSpec((tm, D // 2), lambda m: (m, 0)),
            pl.BlockSpec((tm, 128), lambda m: (m, 0)),
        ],
        out_shape=[
            jax.ShapeDtypeStruct((S, D), F32),
            jax.ShapeDtypeStruct((S, D // 2), I32),
            jax.ShapeDtypeStruct((S, 128), F32),
        ],
        compiler_params=_cparams(("arbitrary",)),
        name="out_proj",
    )(merged, wo, x2, g1, nw, sc, sh, w_router, b_router)


def _pack_halves(h):
    n = h.shape[1] // 2
    bits = pltpu.bitcast(h.astype(BF16).astype(F32), I32)
    lo = lax.shift_right_logical(bits[:, :n], jnp.int32(16))
    hi = bits[:, n:] & jnp.int32(-65536)
    return hi | lo


def _unpack_halves(w):
    lo = pltpu.bitcast(lax.shift_left(w, jnp.int32(16)), F32)
    hi = pltpu.bitcast(w & jnp.int32(-65536), F32)
    return jnp.concatenate([lo, hi], axis=1).astype(BF16)


def _moe_kernel(l, be_ref, nu_ref, first_ref, slot_ref, nxt_ref, xs_ref, w1_hbm, w3_hbm, w2_hbm, o_ref,
                w1s, w3s, w2s, w1b, w3b, w2b, sem):
    b = pl.program_id(0)

    def weight_copies(e, s):
        return (pltpu.make_async_copy(w1_hbm.at[l, e], w1s.at[s], sem.at[s, 0]),
                pltpu.make_async_copy(w3_hbm.at[l, e], w3s.at[s], sem.at[s, 1]),
                pltpu.make_async_copy(w2_hbm.at[l, e], w2s.at[s], sem.at[s, 2]))

    @pl.when(jnp.logical_and(b < nu_ref[0], first_ref[b] == 1))
    def _():
        s = slot_ref[b]
        e = be_ref[b]

        @pl.when(b == 0)
        def _():
            for cp in weight_copies(e, s):
                cp.start()

        for cp in weight_copies(e, s):
            cp.wait()

        @pl.when(nxt_ref[b] >= 0)
        def _():
            for cp in weight_copies(nxt_ref[b], 1 - s):
                cp.start()

        w1b[...] = w1s[s].astype(BF16)
        w3b[...] = w3s[s].astype(BF16)
        w2b[...] = w2s[s].astype(BF16)

    @pl.when(b < nu_ref[0])
    def _():
        xb = _unpack_halves(xs_ref[...])
        a = _dot(xb, w1b[...])
        c = _dot(xb, w3b[...])
        hmid = (a * _sigmoid(a)) * c
        o_ref[...] = _dot(hmid.astype(BF16), w2b[...])

    @pl.when(b >= nu_ref[0])
    def _():
        o_ref[...] = jnp.zeros_like(o_ref)


def _combine_kernel(x_ref, y0_ref, y1_ref, gate_ref, g2_ref, *rest):
    g = gate_ref[...]
    y = y0_ref[...] * g[:, 0:1] + y1_ref[...] * g[:, 1:2]
    xo = x_ref[...] + g2_ref[...] * y
    if len(rest) == 1:
        rest[0][...] = xo
    else:
        nw_ref, sc_ref, sh_ref, ws_ref, o_ref, h_ref, zs_ref = rest
        o_ref[...] = xo
        _mixer_input(xo, nw_ref, sc_ref, sh_ref, ws_ref, h_ref, zs_ref)


def _combine_call(xn, y0, y1, gate, g2, nxt=None):
    S, D = xn.shape
    tm = 512
    blk = pl.BlockSpec((tm, D), lambda m: (m, 0))
    row = pl.BlockSpec((1, D), lambda m: (0, 0))
    in_specs = [blk, blk, blk, pl.BlockSpec((tm, TOP_K), lambda m: (m, 0)), row]
    args = [xn, y0, y1, gate, g2]
    out_specs, out_shape = blk, jax.ShapeDtypeStruct((S, D), F32)
    if nxt is not None:
        ln, nw, sc, sh, w_small = nxt
        in_specs += [row, row, row, pl.BlockSpec((1, D, 128), lambda m: (ln, 0, 0))]
        args += [nw, sc, sh, w_small]
        out_specs = [blk, blk, pl.BlockSpec((tm, 128), lambda m: (m, 0))]
        out_shape = [out_shape, jax.ShapeDtypeStruct((S, D), BF16), jax.ShapeDtypeStruct((S, 128), F32)]
    return pl.pallas_call(
        _combine_kernel,
        grid=(S // tm,),
        in_specs=in_specs,
        out_specs=out_specs,
        out_shape=out_shape,
        compiler_params=_cparams(("arbitrary",)),
        name="moe_combine",
    )(*args)


def _moe_call(l, plan, xs, w1, w3, w2):
    rows = xs.shape[0]
    nblk = rows // MOE_BLK
    D = D_MODEL
    hbm = pl.BlockSpec(memory_space=pl.ANY)
    grid_spec = pltpu.PrefetchScalarGridSpec(
        num_scalar_prefetch=len(plan),
        grid=(nblk,),
        in_specs=[pl.BlockSpec((MOE_BLK, D // 2), lambda b, *_: (b, 0)), hbm, hbm, hbm],
        out_specs=pl.BlockSpec((MOE_BLK, D), lambda b, *_: (b, 0)),
        scratch_shapes=[pltpu.VMEM((2, D, D_FF_E), F32), pltpu.VMEM((2, D, D_FF_E), F32),
                        pltpu.VMEM((2, D_FF_E, D), F32),
                        pltpu.VMEM((D, D_FF_E), BF16), pltpu.VMEM((D, D_FF_E), BF16),
                        pltpu.VMEM((D_FF_E, D), BF16),
                        pltpu.SemaphoreType.DMA((2, 3))],
    )
    return pl.pallas_call(
        functools.partial(_moe_kernel, l),
        grid_spec=grid_spec,
        out_shape=jax.ShapeDtypeStruct((rows, D), F32),
        compiler_params=_cparams(("arbitrary",)),
        name="moe_ffn",
    )(*plan, xs, w1, w3, w2)


def _route(logits):
    n = logits.shape[0]
    g_logit = logits[:, :N_GROUPS]
    g_sel = jnp.argmax(g_logit, axis=-1)
    g_oh = g_sel[:, None] == jnp.arange(N_GROUPS)[None, :]
    p_grp = jnp.sum(jnp.where(g_oh, jax.nn.softmax(g_logit, axis=-1), 0.0), axis=-1, keepdims=True)
    e_logit = logits[:, N_GROUPS:N_GROUPS + N_EXPERTS].reshape(n, N_GROUPS, EXP_PER_GROUP)
    e_logit = jnp.sum(jnp.where(g_oh[:, :, None], e_logit, 0.0), axis=1)
    top_v, top_i = lax.top_k(e_logit, TOP_K)
    gate = p_grp * jax.nn.softmax(top_v, axis=-1)
    expert = (g_sel[:, None] * EXP_PER_GROUP + top_i).reshape(-1).astype(I32)
    m = n * TOP_K
    onehot = (expert[:, None] == jnp.arange(N_EXPERTS, dtype=I32)[None, :]).astype(I32)
    rank = jnp.sum((jnp.cumsum(onehot, axis=0) - onehot) * onehot, axis=1)
    counts = jnp.sum(onehot, axis=0)
    pcounts = (counts + MOE_BLK - 1) // MOE_BLK * MOE_BLK
    pends = jnp.cumsum(pcounts)
    dest = (pends - pcounts)[expert] + rank
    nblk = -(-m // MOE_BLK) + N_EXPERTS
    rows = nblk * MOE_BLK
    tok = jnp.arange(m, dtype=I32) // TOP_K
    row_tok = (jnp.arange(rows, dtype=I32) % n).at[dest].set(tok)
    blk_start = jnp.arange(nblk, dtype=I32)[:, None] * MOE_BLK
    blk_exp = jnp.minimum(jnp.sum((pends[None, :] <= blk_start).astype(I32), axis=1),
                          N_EXPERTS - 1).astype(I32)
    nused = (pends[-1] // MOE_BLK).astype(I32).reshape(1)
    prev = jnp.concatenate([jnp.full((1,), -1, I32), blk_exp[:-1]])
    first = (blk_exp != prev).astype(I32)
    slot = (jnp.cumsum(first) - 1) & 1
    eid = jnp.arange(N_EXPERTS, dtype=I32)
    later = jnp.logical_and(eid[None, :] > eid[:, None], (counts > 0)[None, :])
    nxt_e = jnp.min(jnp.where(later, eid[None, :], N_EXPERTS), axis=1)
    nxt = jnp.where(nxt_e < N_EXPERTS, nxt_e, -1)[blk_exp].astype(I32)
    plan = (blk_exp, nused, first, slot.astype(I32), nxt)
    return row_tok, gate, plan, dest.reshape(n, TOP_K)


def _layer(l, x2, h, zs, mod, nxt, norm2, w_big, conv_w, bg_row, mlstm_norm, q_norm, k_norm,
           tz, bfar, wpm, wpa, wo, w_router, b_router, w1, w3, w2):
    D = D_MODEL
    sh1, sc1, g1, sh2, sc2, g2 = [mod[:, j * D:(j + 1) * D] for j in range(6)]
    z = _in_call(l, h, w_big)
    ym = _mlstm_call(z, zs, conv_w, bg_row, mlstm_norm[None, :])
    kan, vta, kie, kio = _kvprep_call(z, zs, k_norm[None, :])
    ya = _dsa_call(z, zs, kie, kio, kan, vta, q_norm[None, :], tz, bfar)
    merged = _merge_call(l, ym, ya, wpm, wpa, z)
    xn, h2, logits = _out_call(l, merged, wo, x2, g1, norm2[None, :], sc2, sh2, w_router, b_router)
    row_tok, gate, plan, dest = _route(logits)
    xs = h2[row_tok]
    yb = _moe_call(l, plan, xs, w1, w3, w2)
    return _combine_call(xn, yb[dest[:, 0]], yb[dest[:, 1]], gate, g2, nxt)


def kernel(x, c, w_ada, b_ada, norm1, norm2, w_in, conv_w, b_gate, mlstm_norm, q_norm, k_norm,
           rel_bias, w_proj_m, w_proj_a, w_out, w_grp, b_grp, w_exp, b_exp, w1, w3, w2):
    B, S, D = x.shape
    assert B == 1 and D == D_MODEL
    x2 = x.reshape(S, D)
    c16 = jnp.pad(c, ((0, 16 - B), (0, 0)))
    mods = _ada_call(c16, w_ada, b_ada[:, None, :])
    tz, bfar = _bias_tables(rel_bias)
    w_big, w_small = _wprep_call(jnp.swapaxes(w_in, 1, 2))
    wpm = w_proj_m.astype(BF16)
    wpa = w_proj_a.astype(BF16)
    wo = w_out.astype(BF16)

    def norm1_params(l):
        return (l, norm1[l][None, :], mods[l, 0:1, D:2 * D], mods[l, 0:1, 0:D], w_small)

    first = norm1_params(0)
    h, zs = _prenorm_call(first[0], x2, *first[1:])
    for l in range(DEPTH):
        bg_row = jnp.zeros((1, 128), F32).at[0, S_IP:S_IP + 2 * NH_M].set(b_gate[l])
        w_router = jnp.concatenate([w_grp[l], w_exp[l],
                                    jnp.zeros((D, 128 - N_GROUPS - N_EXPERTS), F32)], axis=1)
        b_router = jnp.concatenate([b_grp[l], b_exp[l],
                                    jnp.zeros((128 - N_GROUPS - N_EXPERTS,), F32)])[None, :]
        nxt = norm1_params(l + 1) if l + 1 < DEPTH else None
        out = _layer(l, x2, h, zs, mods[l, 0:1, :], nxt, norm2[l], w_big, conv_w[l], bg_row,
                     mlstm_norm[l], q_norm[l], k_norm[l], tz, bfar, wpm, wpa, wo,
                     w_router, b_router, w1, w3, w2)
        if nxt is None:
            x2 = out
        else:
            x2, h, zs = out
    return x2.reshape(B, S, D)
```
